```python
import jax, jax.numpy as jnp
from jax import lax
import numpy as np

D_MODEL = 1024
BATCH = 1
SEQ = 16384
DEPTH = 1
DEC_BATCH = 128
DEC_SEQ = 1
PAST_LEN = 8192
PAGE_SIZE = 128

HEAD_DIM = 64
A_HEADS = 8
A_WIDTH = A_HEADS * HEAD_DIM
B_GROUPS = 8
B_GROUP_DIM = 64
B_WIDTH = B_GROUPS * B_GROUP_DIM
MIX_WIDTH = A_WIDTH + B_WIDTH
IN_COLS = 3 * A_WIDTH + 2 * B_WIDTH
PATTERNS = ((128, 1), (512, 4), (2048, 16))
MAX_WINDOW = 2048
STRIDED_BLOCK = 128
ROT_DIM = HEAD_DIM // 4
ROPE_THETA = 500000.0
CHUNK = 128
PEER_HEADS = 8
PEER_NKEYS = 128
PEER_EXPERTS = PEER_NKEYS * PEER_NKEYS
PEER_QDIM = 256
PEER_HALF = PEER_QDIM // 2
PEER_TOPK = 16
PEER_BLOCK = 128
PLE_DIM = 256
EPS = 1e-6

kernel_name = "hymba_dilated_gmlp_peer_step"


def rmsnorm(x, g):
    xf = x.astype(jnp.float32)
    y = xf * lax.rsqrt(jnp.mean(xf * xf, axis=-1, keepdims=True) + EPS)
    return (y * g.astype(jnp.float32)).astype(x.dtype)


def rope(x, pos):
    half = ROT_DIM // 2
    inv = ROPE_THETA ** (-jnp.arange(0, ROT_DIM, 2, dtype=jnp.float32) / ROT_DIM)
    ang = pos.astype(jnp.float32)[:, None] * inv[None, :]
    cos = jnp.cos(ang)[:, None, :]
    sin = jnp.sin(ang)[:, None, :]
    xr = x[..., :ROT_DIM].astype(jnp.float32)
    x1, x2 = xr[..., :half], xr[..., half:]
    rot = jnp.concatenate([x1 * cos - x2 * sin, x2 * cos + x1 * sin], axis=-1)
    return jnp.concatenate([rot.astype(x.dtype), x[..., ROT_DIM:]], axis=-1)


def project(n, w_in, q_norm_g, k_norm_g, pos):
    bsz, t, _ = n.shape
    z = n @ w_in
    q, k, v, u_b, v_b = jnp.split(z, [A_WIDTH, 2 * A_WIDTH, 3 * A_WIDTH, 3 * A_WIDTH + B_WIDTH], axis=-1)
    heads = lambda a: a.reshape(bsz, t, A_HEADS, HEAD_DIM)
    q = rope(rmsnorm(heads(q), q_norm_g), pos)
    k = rope(rmsnorm(heads(k), k_norm_g), pos)
    return q, k, heads(v), jax.nn.gelu(u_b), jax.nn.gelu(v_b)


def combine_patterns(outs, lses):
    alpha = jax.nn.softmax(jnp.stack(lses, axis=0), axis=0)
    return jnp.sum(alpha[..., None] * jnp.stack(outs, axis=0), axis=0)


def _to_strided(x, dil, s_pad):
    b, s, h, c = x.shape
    x = jnp.pad(x, ((0, 0), (0, s_pad - s), (0, 0), (0, 0)))
    length = s_pad // dil
    x = x.reshape(b, length, dil, h, c).transpose(0, 2, 3, 1, 4)
    return x.reshape(b, dil, h, length // STRIDED_BLOCK, STRIDED_BLOCK, c)


def _with_prev(t):
    prev = jnp.pad(t[:, :, :, :-1], ((0, 0), (0, 0), (0, 0), (1, 0), (0, 0), (0, 0)))
    return jnp.concatenate([prev, t], axis=4)


def dilated_attention_prompt(q, k, v):
    b, s, h, c = q.shape
    scale = c ** -0.5
    qi = jnp.arange(STRIDED_BLOCK)[:, None]
    kj = jnp.arange(2 * STRIDED_BLOCK)[None, :]
    dist = STRIDED_BLOCK + qi - kj
    outs, lses = [], []
    for window, dil in PATTERNS:
        n_back = window // dil
        unit = dil * STRIDED_BLOCK
        s_pad = -(-s // unit) * unit
        nb = s_pad // unit
        qs = _to_strided(q, dil, s_pad).astype(jnp.float32)
        kc = _with_prev(_to_strided(k, dil, s_pad)).astype(jnp.float32)
        vc = _with_prev(_to_strided(v, dil, s_pad)).astype(jnp.float32)
        scores = jnp.einsum('bdhnqc,bdhnkc->bdhnqk', qs, kc) * scale
        blk = jnp.arange(nb)[:, None, None]
        valid = (dist >= 0) & (dist <= n_back) & ((blk - 1) * STRIDED_BLOCK + kj >= 0)
        scores = jnp.where(valid, scores, -jnp.inf)
        m = jnp.max(scores, axis=-1, keepdims=True)
        e = jnp.exp(scores - m)
        den = jnp.sum(e, axis=-1)
        o = jnp.einsum('bdhnqk,bdhnkc->bdhnqc', e, vc) / den[..., None]
        lse = m[..., 0] + jnp.log(den)
        length = s_pad // dil
        o = o.reshape(b, dil, h, length, c).transpose(0, 3, 1, 2, 4).reshape(b, s_pad, h, c)[:, :s]
        lse = lse.reshape(b, dil, h, length).transpose(0, 3, 1, 2).reshape(b, s_pad, h)[:, :s]
        outs.append(o)
        lses.append(lse)
    return combine_patterns(outs, lses).astype(q.dtype)


def dilated_attention_sample(q, k_all, v_all, buf_start):
    t = q.shape[1]
    scale = q.shape[-1] ** -0.5
    t_idx = jnp.arange(t)[:, None]
    qf = q.astype(jnp.float32)
    outs, lses = [], []
    for window, dil in PATTERNS:
        n_back = window // dil
        kk = jnp.arange(n_back + 1)[None, :]
        pos = PAST_LEN + t_idx - kk * dil
        idx = pos - buf_start
        valid = (pos >= 0) & (idx >= 0)
        idx_c = jnp.clip(idx, 0, k_all.shape[1] - 1)
        kg = jnp.take(k_all, idx_c, axis=1).astype(jnp.float32)
        vg = jnp.take(v_all, idx_c, axis=1).astype(jnp.float32)
        scores = jnp.einsum('bthc,btkhc->bthk', qf, kg) * scale
        scores = jnp.where(valid[None, :, None, :], scores, -jnp.inf)
        m = jnp.max(scores, axis=-1, keepdims=True)
        e = jnp.exp(scores - m)
        den = jnp.sum(e, axis=-1)
        o = jnp.einsum('bthk,btkhc->bthc', e, vg) / den[..., None]
        outs.append(o)
        lses.append(m[..., 0] + jnp.log(den))
    return combine_patterns(outs, lses).astype(q.dtype)


def spatial_gate(u, v, w_s, b_s, v_norm_g):
    bsz, t, _ = v.shape
    vn = rmsnorm(v.reshape(bsz, t, B_GROUPS, B_GROUP_DIM), v_norm_g)
    t_pad = -(-t // CHUNK) * CHUNK
    vc = jnp.pad(vn, ((0, 0), (0, t_pad - t), (0, 0), (0, 0))).reshape(bsz, t_pad // CHUNK, CHUNK, B_GROUPS, B_GROUP_DIM)
    wm = w_s * jnp.tril(jnp.ones((CHUNK, CHUNK), dtype=w_s.dtype))
    mixed = jnp.einsum('gij,bnjgc->bnigc', wm, vc) + b_s.T[:, :, None]
    mixed = mixed.reshape(bsz, t_pad, B_GROUPS, B_GROUP_DIM)[:, :t]
    out = u.reshape(bsz, t, B_GROUPS, B_GROUP_DIM) * mixed
    return out.reshape(bsz, t, B_WIDTH), vn


def merge_groups(att, gm, out_norm_a_g, out_norm_b_g, w_out):
    bsz, t = att.shape[:2]
    a = rmsnorm(att.reshape(bsz, t, A_WIDTH), out_norm_a_g)
    b = rmsnorm(gm, out_norm_b_g)
    return jnp.concatenate([a, b], axis=-1) @ w_out


def peer(x, w_query, sub_keys, expert_u, expert_v):
    n_tok = x.shape[0]
    n_pad = -(-n_tok // PEER_BLOCK) * PEER_BLOCK
    xb = jnp.pad(x, ((0, n_pad - n_tok), (0, 0))).reshape(n_pad // PEER_BLOCK, PEER_BLOCK, D_MODEL)

    def block(xt):
        q = (xt @ w_query).reshape(PEER_BLOCK, PEER_HEADS, 2, PEER_HALF).astype(jnp.float32)
        s = jnp.einsum('thpc,hpkc->thpk', q, sub_keys.astype(jnp.float32))
        top_s, top_i = lax.top_k(s, PEER_TOPK)
        cand_s = top_s[:, :, 0, :, None] + top_s[:, :, 1, None, :]
        cand_i = top_i[:, :, 0, :, None] * PEER_NKEYS + top_i[:, :, 1, None, :]
        cand_s = cand_s.reshape(PEER_BLOCK, PEER_HEADS, PEER_TOPK * PEER_TOPK)
        cand_i = cand_i.reshape(PEER_BLOCK, PEER_HEADS, PEER_TOPK * PEER_TOPK)
        best_s, best_pos = lax.top_k(cand_s, PEER_TOPK)
        eidx = jnp.take_along_axis(cand_i, best_pos, axis=-1)
        g = jax.nn.softmax(best_s, axis=-1)
        act = jax.nn.gelu(jnp.einsum('thkd,td->thk', expert_u[eidx], xt).astype(jnp.float32))
        return jnp.einsum('thk,thkd->td', (g * act).astype(x.dtype), expert_v[eidx])

    out = lax.map(block, xb)
    return out.reshape(n_pad, D_MODEL)[:n_tok]


def channel_and_ple(h, p, norm2_g, w_query, sub_keys, expert_u, expert_v, norm3_g, w_gate, w_proj):
    bsz, t, _ = h.shape
    n2 = rmsnorm(h, norm2_g)
    h = h + peer(n2.reshape(bsz * t, D_MODEL), w_query, sub_keys, expert_u, expert_v).reshape(bsz, t, D_MODEL)
    n3 = rmsnorm(h, norm3_g)
    return h + jax.nn.sigmoid(n3 @ w_gate) * (p.astype(h.dtype) @ w_proj)


def setup_inputs(seed: int = 0) -> dict:
    key = jax.random.key(seed)
    ks = jax.random.split(key, 24)
    nrm = lambda k, shape, sc: jax.random.normal(k, shape, jnp.float32) * sc
    gain = lambda k, shape: 1.0 + 0.02 * jax.random.normal(k, shape, jnp.float32)
    buf = min(MAX_WINDOW, PAST_LEN)
    return {
        "x_prompt": nrm(ks[0], (BATCH, SEQ, D_MODEL), 1.0),
        "x_sample": nrm(ks[1], (DEC_BATCH, DEC_SEQ, D_MODEL), 1.0),
        "cache_k": nrm(ks[2], (DEPTH, DEC_BATCH, buf, A_HEADS, HEAD_DIM), 1.0),
        "cache_v": nrm(ks[3], (DEPTH, DEC_BATCH, buf, A_HEADS, HEAD_DIM), 1.0),
        "p_prompt": nrm(ks[4], (DEPTH, BATCH, SEQ, PLE_DIM), 1.0),
        "p_sample": nrm(ks[5], (DEPTH, DEC_BATCH, DEC_SEQ, PLE_DIM), 1.0),
        "norm1_g": gain(ks[6], (DEPTH, D_MODEL)),
        "w_in": nrm(ks[7], (DEPTH, D_MODEL, IN_COLS), D_MODEL ** -0.5),
        "q_norm_g": gain(ks[8], (DEPTH, HEAD_DIM)),
        "k_norm_g": gain(ks[9], (DEPTH, HEAD_DIM)),
        "v_norm_g": gain(ks[10], (DEPTH, B_GROUPS, B_GROUP_DIM)),
        "spatial_w": nrm(ks[11], (DEPTH, B_GROUPS, CHUNK, CHUNK), CHUNK ** -0.5),
        "spatial_b": gain(ks[12], (DEPTH, B_GROUPS, CHUNK)),
        "out_norm_a_g": gain(ks[13], (DEPTH, A_WIDTH)),
        "out_norm_b_g": gain(ks[14], (DEPTH, B_WIDTH)),
        "w_out": nrm(ks[15], (DEPTH, MIX_WIDTH, D_MODEL), MIX_WIDTH ** -0.5),
        "norm2_g": gain(ks[16], (DEPTH, D_MODEL)),
        "peer_w_query": nrm(ks[17], (DEPTH, D_MODEL, PEER_HEADS * PEER_QDIM), D_MODEL ** -0.5),
        "peer_sub_keys": nrm(ks[18], (DEPTH, PEER_HEADS, 2, PEER_NKEYS, PEER_HALF), PEER_HALF ** -0.5),
        "peer_u": nrm(ks[19], (DEPTH, PEER_EXPERTS, D_MODEL), D_MODEL ** -0.5),
        "peer_v": nrm(ks[20], (DEPTH, PEER_EXPERTS, D_MODEL), PEER_HEADS ** -0.5),
        "norm3_g": gain(ks[21], (DEPTH, D_MODEL)),
        "ple_w_gate": nrm(ks[22], (DEPTH, D_MODEL, D_MODEL), D_MODEL ** -0.5),
        "ple_w_proj": nrm(ks[23], (DEPTH, PLE_DIM, D_MODEL), PLE_DIM ** -0.5),
    }


def reference(x_prompt, x_sample, cache_k, cache_v, p_prompt, p_sample, norm1_g, w_in, q_norm_g, k_norm_g, v_norm_g, spatial_w, spatial_b, out_norm_a_g, out_norm_b_g, w_out, norm2_g, peer_w_query, peer_sub_keys, peer_u, peer_v, norm3_g, ple_w_gate, ple_w_proj):
    s = x_prompt.shape[1]
    t = x_sample.shape[1]
    pos_p = jnp.arange(s, dtype=jnp.int32)
    pos_s = PAST_LEN + jnp.arange(t, dtype=jnp.int32)
    keep = min(MAX_WINDOW, s)
    buf_start = PAST_LEN - cache_k.shape[2]
    hp, hs = x_prompt, x_sample
    nkp, nvp, nks, nvs, ncv = [], [], [], [], []
    for l in range(DEPTH):
        q, k, v, ub, vb = project(rmsnorm(hp, norm1_g[l]), w_in[l], q_norm_g[l], k_norm_g[l], pos_p)
        att = dilated_attention_prompt(q, k, v)
        gm, _ = spatial_gate(ub, vb, spatial_w[l], spatial_b[l], v_norm_g[l])
        hp = hp + merge_groups(att, gm, out_norm_a_g[l], out_norm_b_g[l], w_out[l])
        hp = channel_and_ple(hp, p_prompt[l], norm2_g[l], peer_w_query[l], peer_sub_keys[l], peer_u[l], peer_v[l], norm3_g[l], ple_w_gate[l], ple_w_proj[l])
        nkp.append(k[:, s - keep:])
        nvp.append(v[:, s - keep:])
        q, k, v, ub, vb = project(rmsnorm(hs, norm1_g[l]), w_in[l], q_norm_g[l], k_norm_g[l], pos_s)
        k_all = jnp.concatenate([cache_k[l].astype(k.dtype), k], axis=1)
        v_all = jnp.concatenate([cache_v[l].astype(v.dtype), v], axis=1)
        att = dilated_attention_sample(q, k_all, v_all, buf_start)
        gm, vn = spatial_gate(ub, vb, spatial_w[l], spatial_b[l], v_norm_g[l])
        hs = hs + merge_groups(att, gm, out_norm_a_g[l], out_norm_b_g[l], w_out[l])
        hs = channel_and_ple(hs, p_sample[l], norm2_g[l], peer_w_query[l], peer_sub_keys[l], peer_u[l], peer_v[l], norm3_g[l], ple_w_gate[l], ple_w_proj[l])
        nks.append(k)
        nvs.append(v)
        ncv.append(vn)
    return (hp, hs, jnp.stack(nkp), jnp.stack(nvp), jnp.stack(nks), jnp.stack(nvs), jnp.stack(ncv))
```

```python
import functools

import jax
import jax.numpy as jnp
from jax import lax
from jax.experimental import pallas as pl
from jax.experimental.pallas import tpu as pltpu

HEAD_DIM = 64
A_HEADS = 8
A_WIDTH = A_HEADS * HEAD_DIM
B_GROUPS = 8
B_WIDTH = B_GROUPS * HEAD_DIM
PATTERNS = ((128, 1), (512, 4), (2048, 16))
MAX_WINDOW = 2048
PAST_LEN = 8192
BLK = 128
ROT_DIM = HEAD_DIM // 4
ROPE_THETA = 500000.0
PEER_HEADS = 8
PEER_NKEYS = 128
PEER_TOPK = 16
EPS = 1e-6
LANES = 128

VMEM_LIMIT = 48 * 1024 * 1024

F32 = jnp.float32
BF16 = jnp.bfloat16
NEG_INF = float("-inf")


def _params(*sem):
    return pltpu.CompilerParams(dimension_semantics=sem, vmem_limit_bytes=VMEM_LIMIT)


def _dot(a, b):
    return jnp.dot(a, b, preferred_element_type=F32)


def _dot_split(a, m, passes):
    acc = None
    rem = a
    for _ in range(passes):
        piece = rem.astype(BF16)
        rem = rem - piece.astype(F32)
        t = _dot(piece, m)
        acc = t if acc is None else acc + t
    return acc


def _rms(x, g):
    return x * lax.rsqrt(jnp.mean(x * x, axis=-1, keepdims=True) + EPS) * g


def _tile4(t):
    return jnp.concatenate([t, t, t, t], axis=1)


def _proj_kernel(x_ref, g1_ref, win_ref, qg_ref, kg_ref, vg_ref, c_ref, s1_ref, s2_ref, mh_ref, *rest, sample):
    if sample:
        w00_ref, b0_ref, q_out, k_out, v_out, gm_out, vn_out = rest
    else:
        ws_ref, bias_ref, q_out, k_out, v_out, gm_out = rest
    tm = x_ref.shape[0]
    n1 = _rms(x_ref[...], g1_ref[...])
    z = _dot(n1.astype(BF16), win_ref[...])
    mh = mh_ref[...]
    cos_t, sin_a, sin_b = _tile4(c_ref[...]), _tile4(s1_ref[...]), _tile4(s2_ref[...])

    def head_norm_rope(t, g):
        tn = t * lax.rsqrt(_dot_split(t * t, mh, 2) + EPS) * g
        return tn * cos_t + pltpu.roll(tn, A_WIDTH - ROT_DIM // 2, 1) * sin_a + pltpu.roll(tn, ROT_DIM // 2, 1) * sin_b

    q_out[...] = head_norm_rope(z[:, 0:A_WIDTH], qg_ref[...])
    k_out[...] = head_norm_rope(z[:, A_WIDTH : 2 * A_WIDTH], kg_ref[...])
    v_out[...] = z[:, 2 * A_WIDTH : 3 * A_WIDTH]
    u = jax.nn.gelu(z[:, 3 * A_WIDTH : 3 * A_WIDTH + B_WIDTH])
    gv = jax.nn.gelu(z[:, 3 * A_WIDTH + B_WIDTH :])
    vn = gv * lax.rsqrt(_dot_split(gv * gv, mh, 2) + EPS) * vg_ref[...]
    if sample:
        vn_out[...] = vn
        gm_out[...] = u * (vn * w00_ref[...] + b0_ref[...])
    else:
        row = lax.broadcasted_iota(jnp.int32, (BLK, BLK), 0)
        col = lax.broadcasted_iota(jnp.int32, (BLK, BLK), 1)
        lane_lo = col < HEAD_DIM
        causal = row >= col
        wm = [jnp.where(causal, ws_ref[g], 0.0).astype(BF16) for g in range(B_GROUPS)]
        vnb = vn.astype(BF16)
        for c in range(tm // BLK):
            pieces = []
            for j in range(B_WIDTH // LANES):
                s = vnb[c * BLK : (c + 1) * BLK, j * LANES : (j + 1) * LANES]
                pieces.append(jnp.where(lane_lo, _dot(wm[2 * j], s), _dot(wm[2 * j + 1], s)))
            mixed = jnp.concatenate(pieces, axis=1) + bias_ref[...]
            gm_out[c * BLK : (c + 1) * BLK, :] = u[c * BLK : (c + 1) * BLK, :] * mixed


def _proj(x, pos, g1, w_in, qg, kg, vg, ws, bs, sample):
    t, d = x.shape
    tm = min(t, 256)
    half = ROT_DIM // 2
    inv = ROPE_THETA ** (-jnp.arange(0, ROT_DIM, 2, dtype=F32) / ROT_DIM)
    ang = pos.astype(F32)[:, None] * inv[None, :]
    cos, sin = jnp.cos(ang), jnp.sin(ang)
    zeros = lambda n: jnp.zeros((t, n), F32)
    rest = HEAD_DIM - ROT_DIM
    c64 = jnp.concatenate([cos, cos, jnp.ones((t, rest), F32)], axis=1)
    sa64 = jnp.concatenate([-sin, zeros(half + rest)], axis=1)
    sb64 = jnp.concatenate([zeros(half), sin, zeros(rest)], axis=1)
    two = lambda a: jnp.concatenate([a, a], axis=1)
    head_id = jnp.arange(A_WIDTH) // HEAD_DIM
    mh = ((head_id[:, None] == head_id[None, :]).astype(F32) / HEAD_DIM).astype(BF16)
    row = lambda a: a.reshape(1, -1)
    tiled = lambda a: jnp.tile(a, A_HEADS).reshape(1, -1)
    full = lambda a: pl.BlockSpec(a.shape, lambda i: (0,) * a.ndim)
    tok = lambda n: pl.BlockSpec((tm, n), lambda i: (i, 0))
    args = [x, row(g1), w_in.astype(BF16), tiled(qg), tiled(kg), row(vg), two(c64), two(sa64), two(sb64), mh]
    specs = [tok(d)] + [full(a) for a in args[1:6]] + [tok(LANES)] * 3 + [full(mh)]
    if sample:
        extra = [jnp.repeat(ws[:, 0, 0], HEAD_DIM).reshape(1, -1), jnp.repeat(bs[:, 0], HEAD_DIM).reshape(1, -1)]
        n_out = 5
    else:
        extra = [ws, jnp.repeat(bs.T, HEAD_DIM, axis=1)]
        n_out = 4
    args += extra
    specs += [full(a) for a in extra]
    return pl.pallas_call(
        functools.partial(_proj_kernel, sample=sample),
        grid=(t // tm,),
        in_specs=specs,
        out_specs=[tok(A_WIDTH)] * n_out,
        out_shape=[jax.ShapeDtypeStruct((t, A_WIDTH), F32)] * n_out,
        compiler_params=_params("parallel"),
        name="proj_sample" if sample else "proj_prompt",
    )(*args)


def _attn_kernel(q_ref, k_ref, kp_ref, v_ref, vp_ref, o_ref, l_ref, *, nsub):
    first = pl.program_id(1) == 0
    qi = lax.broadcasted_iota(jnp.int32, (BLK, 2 * BLK), 0)
    kj = lax.broadcasted_iota(jnp.int32, (BLK, 2 * BLK), 1)
    dist = BLK + qi - kj
    band = (dist >= 0) & (dist <= BLK)
    band_first = band & ((kj >= BLK) | jnp.logical_not(first))
    lane_lo = lax.broadcasted_iota(jnp.int32, (BLK, LANES), 1) < HEAD_DIM
    scale = HEAD_DIM**-0.5
    for sb in range(nsub):
        r0 = sb * BLK
        rows = slice(r0, r0 + BLK)
        prev = slice(r0 - BLK, r0)
        valid = band_first if sb == 0 else band
        for j in range(A_WIDTH // LANES):
            cols = slice(j * LANES, (j + 1) * LANES)
            qf = q_ref[rows, cols]
            kprev = kp_ref[:, cols] if sb == 0 else k_ref[prev, cols]
            vprev = vp_ref[:, cols] if sb == 0 else v_ref[prev, cols]
            k2 = jnp.concatenate([kprev, k_ref[rows, cols]], axis=0).astype(BF16)
            v2 = jnp.concatenate([vprev, v_ref[rows, cols]], axis=0).astype(BF16)
            halves = []
            for lo in (True, False):
                qm = jnp.where(lane_lo == lo, qf, 0.0).astype(BF16)
                s = lax.dot_general(qm, k2, (((1,), (1,)), ((), ())), preferred_element_type=F32) * scale
                s = jnp.where(valid, s, NEG_INF)
                m = jnp.max(s, axis=1, keepdims=True)
                e = jnp.exp(s - m)
                den = jnp.sum(e, axis=1, keepdims=True)
                halves.append((_dot(e.astype(BF16), v2) / den, m + jnp.log(den)))
            o_ref[rows, cols] = jnp.where(lane_lo, halves[0][0], halves[1][0])
            l_ref[rows, cols] = jnp.where(lane_lo, halves[0][1], halves[1][1])


def _attn_prompt(q, k, v, dil):
    s = q.shape[0]
    length = s // dil
    qb = min(length, 512)
    nsub = qb // BLK
    view = lambda a: a.reshape(length, dil * A_WIDTH)
    own = pl.BlockSpec((qb, A_WIDTH), lambda r, b: (b, r))
    prev = pl.BlockSpec((BLK, A_WIDTH), lambda r, b: (jnp.maximum(b * nsub - 1, 0), r))
    o, lse = pl.pallas_call(
        functools.partial(_attn_kernel, nsub=nsub),
        grid=(dil, length // qb),
        in_specs=[own, own, prev, own, prev],
        out_specs=[own, own],
        out_shape=[jax.ShapeDtypeStruct((length, dil * A_WIDTH), F32)] * 2,
        compiler_params=_params("parallel", "arbitrary"),
        name=f"attn_d{dil}",
    )(view(q), view(k), view(k), view(v), view(v))
    return o.reshape(s, A_WIDTH), lse.reshape(s, A_WIDTH)


def _attn_sample_kernel(q_ref, ks_ref, vs_ref, k1_ref, k4_ref, k16_ref, v1_ref, v4_ref, v16_ref, hs_ref, hst_ref, o_ref):
    bb = q_ref.shape[0]
    hs = hs_ref[...]
    hst = hst_ref[...]
    scale = HEAD_DIM**-0.5
    n_pat = len(PATTERNS)
    up8 = lambda a: jnp.broadcast_to(a, (8, a.shape[1]))

    def body(bi, carry):
        q = q_ref[pl.ds(bi, 1), :]
        ks = ks_ref[pl.ds(bi, 1), :]
        vs = vs_ref[pl.ds(bi, 1), :]
        s_self = _dot_split(up8(ks * q), hs, 3)[0:1] * scale
        s_pat = [_dot_split(kr[bi] * q, hs, 2) * scale for kr in (k1_ref, k4_ref, k16_ref)]
        m = s_self
        for sp in s_pat:
            m = jnp.maximum(m, jnp.max(sp, axis=0, keepdims=True))
        e_self = jnp.exp(s_self - m)
        e_pat = [jnp.exp(sp - m) for sp in s_pat]
        den = n_pat * e_self
        for ep in e_pat:
            den = den + jnp.sum(ep, axis=0, keepdims=True)
        inv = 1.0 / den
        acc = _dot_split(up8(n_pat * e_self * inv), hst, 3)[0:1] * vs
        for ep, vr in zip(e_pat, (v1_ref, v4_ref, v16_ref)):
            acc = acc + jnp.sum(_dot((ep * inv).astype(BF16), hst) * vr[bi], axis=0, keepdims=True)
        o_ref[pl.ds(bi, 1), :] = acc
        return carry

    lax.fori_loop(0, bb, body, 0)


def _attn_sample(q, k, v, ck, cv):
    b, buf = ck.shape[0], ck.shape[1]
    bb = min(b, 8)
    views, specs = [], []
    for cache in (ck, cv):
        for window, dil in PATTERNS:
            nrow = buf // dil
            views.append(cache.reshape(b, nrow, dil * A_WIDTH))
            specs.append(pl.BlockSpec((bb, BLK, A_WIDTH), functools.partial(lambda i, nb: (i, nb, 0), nb=nrow // BLK - 1)))
    head_id = jnp.arange(A_WIDTH) // HEAD_DIM
    hs = (head_id[:, None] == jnp.arange(LANES)[None, :]).astype(BF16)
    tok = pl.BlockSpec((bb, A_WIDTH), lambda i: (i, 0))
    full = lambda a: pl.BlockSpec(a.shape, lambda i: (0, 0))
    return pl.pallas_call(
        _attn_sample_kernel,
        grid=(b // bb,),
        in_specs=[tok, tok, tok] + specs + [full(hs), full(hs.T)],
        out_specs=tok,
        out_shape=jax.ShapeDtypeStruct((b, A_WIDTH), F32),
        compiler_params=_params("parallel"),
        name="attn_sample",
    )(q, k, v, *views, hs, hs.T)


def _merge_kernel(*refs, n_pat):
    x_ref = refs[0]
    att_refs = refs[1 : 1 + 2 * n_pat] if n_pat else refs[1:2]
    gm_ref, ga_ref, gb_ref, wout_ref, g2_ref, wq_ref, sk_ref, h_out, xt_out, st_out = refs[-10:]
    if n_pat:
        outs = [r[...] for r in att_refs[:n_pat]]
        lses = [r[...] for r in att_refs[n_pat:]]
        m = functools.reduce(jnp.maximum, lses)
        ws = [jnp.exp(l - m) for l in lses]
        att = sum(w * o for w, o in zip(ws, outs)) / sum(ws)
    else:
        att = att_refs[0][...]
    cat = jnp.concatenate([_rms(att, ga_ref[...]), _rms(gm_ref[...], gb_ref[...])], axis=1)
    h = x_ref[...] + _dot(cat.astype(BF16), wout_ref[...])
    h_out[...] = h
    n2 = _rms(h, g2_ref[...])
    xt_out[...] = n2.T.astype(BF16)
    qt = _dot(n2.astype(BF16), wq_ref[...]).T.astype(BF16)
    for i in range(2 * PEER_HEADS):
        rows = slice(i * PEER_NKEYS, (i + 1) * PEER_NKEYS)
        st_out[rows, :] = _dot(sk_ref[i], qt[rows, :])


def _merge(x, atts, gm, ga, gb, w_out, g2, wq, sub_keys):
    t, d = x.shape
    tm = min(t, 256)
    n_pat = len(atts) // 2
    nq = wq.shape[1]
    sk = sub_keys.reshape(2 * PEER_HEADS, PEER_NKEYS, -1).astype(BF16)
    row = lambda a: a.reshape(1, -1)
    full = lambda a: pl.BlockSpec(a.shape, lambda i: (0,) * a.ndim)
    tok = lambda n: pl.BlockSpec((tm, n), lambda i: (i, 0))
    tokt = lambda n: pl.BlockSpec((n, tm), lambda i: (0, i))
    weights = [row(ga), row(gb), w_out.astype(BF16), row(g2), wq.astype(BF16), sk]
    return pl.pallas_call(
        functools.partial(_merge_kernel, n_pat=n_pat),
        grid=(t // tm,),
        in_specs=[tok(d)] + [tok(A_WIDTH)] * (len(atts) + 1) + [full(a) for a in weights],
        out_specs=[tok(d), tokt(d), tokt(nq)],
        out_shape=[
            jax.ShapeDtypeStruct((t, d), F32),
            jax.ShapeDtypeStruct((d, t), BF16),
            jax.ShapeDtypeStruct((nq, t), F32),
        ],
        compiler_params=_params("parallel"),
        name="merge_prompt" if n_pat else "merge_sample",
    )(x, *atts, gm, *weights)


def _top16(a, iota_k):
    rem = a
    rank = jnp.full(a.shape, PEER_TOPK, jnp.int32)
    vals = []
    for r in range(PEER_TOPK):
        m = jnp.max(rem, axis=0, keepdims=True)
        first = jnp.min(jnp.where(rem == m, iota_k, PEER_NKEYS), axis=0, keepdims=True)
        sel = iota_k == first
        rank = jnp.where(sel, r, rank)
        rem = jnp.where(sel, NEG_INF, rem)
        vals.append(m)
    return jnp.concatenate(vals, axis=0), rank


def _select_kernel(st_ref, na_ref, ea_ref, r2_ref, eb_ref):
    tk = st_ref.shape[1]
    iota_k = lax.broadcasted_iota(jnp.int32, (PEER_NKEYS, tk), 0)
    a = st_ref[0:PEER_NKEYS, :]
    b = st_ref[PEER_NKEYS:, :]
    ta, r1 = _top16(a, iota_k)
    tb, r2 = _top16(b, iota_k)
    cand = jnp.concatenate([ta[k : k + 1, :] + tb for k in range(PEER_TOPK)], axis=0)
    pos = lax.broadcasted_iota(jnp.int32, cand.shape, 0)
    rem = cand
    chosen = jnp.zeros(cand.shape, F32)
    for _ in range(PEER_TOPK):
        m = jnp.max(rem, axis=0, keepdims=True)
        first = jnp.min(jnp.where(rem == m, pos, cand.shape[0]), axis=0, keepdims=True)
        sel = pos == first
        chosen = jnp.where(sel, 1.0, chosen)
        rem = jnp.where(sel, NEG_INF, rem)
    z = jnp.sum(chosen * jnp.exp(cand - cand[0:1, :]), axis=0, keepdims=True)
    n_at = jnp.zeros(a.shape, F32)
    for k in range(PEER_TOPK):
        n_k = jnp.sum(chosen[k * PEER_TOPK : (k + 1) * PEER_TOPK, :], axis=0, keepdims=True)
        n_at = jnp.where(r1 == k, n_k, n_at)
    na_ref[0] = n_at
    ea_ref[0] = jnp.exp(a - ta[0:1, :]) / z
    r2_ref[0] = r2.astype(F32)
    eb_ref[0] = jnp.exp(b - tb[0:1, :])


def _select(st):
    t = st.shape[1]
    tk = min(t, 256)
    out = pl.BlockSpec((1, PEER_NKEYS, tk), lambda i, h: (h, 0, i))
    return pl.pallas_call(
        _select_kernel,
        grid=(t // tk, PEER_HEADS),
        in_specs=[pl.BlockSpec((2 * PEER_NKEYS, tk), lambda i, h: (h, i))],
        out_specs=[out] * 4,
        out_shape=[jax.ShapeDtypeStruct((PEER_HEADS, PEER_NKEYS, t), F32)] * 4,
        compiler_params=_params("parallel", "arbitrary"),
        name="peer_select",
    )(st)


def _peer_kernel(u_ref, xt_ref, vt_ref, na_ref, ea_ref, r2_ref, eb_ref, o_ref, acc_ref, gt_ref):
    j = pl.program_id(1)

    @pl.when(j == 0)
    def _():
        acc_ref[...] = jnp.zeros_like(acc_ref)

    st = _dot(u_ref[...], xt_ref[...])
    for ii in range(u_ref.shape[0] // PEER_NKEYS):
        w = None
        for h in range(PEER_HEADS):
            hit = r2_ref[h] < na_ref[h, ii : ii + 1, :]
            t = jnp.where(hit, eb_ref[h], 0.0) * ea_ref[h, ii : ii + 1, :]
            w = t if w is None else w + t
        rows = slice(ii * PEER_NKEYS, (ii + 1) * PEER_NKEYS)
        gt_ref[rows, :] = (jax.nn.gelu(st[rows, :]) * w).astype(BF16)
    acc_ref[...] += _dot(vt_ref[...], gt_ref[...])

    @pl.when(j == pl.num_programs(1) - 1)
    def _():
        o_ref[...] = acc_ref[...].T


def _peer(xt, u, vt, na, ea, r2, eb):
    d, t = xt.shape
    e = u.shape[0]
    tm = min(t, 512)
    te = 1024
    i1_per = te // PEER_NKEYS
    sel_i1 = pl.BlockSpec((PEER_HEADS, i1_per, tm), lambda i, j: (0, j, i))
    sel_i2 = pl.BlockSpec((PEER_HEADS, PEER_NKEYS, tm), lambda i, j: (0, 0, i))
    return pl.pallas_call(
        _peer_kernel,
        grid=(t // tm, e // te),
        in_specs=[
            pl.BlockSpec((te, d), lambda i, j: (j, 0)),
            pl.BlockSpec((d, tm), lambda i, j: (0, i)),
            pl.BlockSpec((d, te), lambda i, j: (0, j)),
            sel_i1,
            sel_i1,
            sel_i2,
            sel_i2,
        ],
        out_specs=pl.BlockSpec((tm, d), lambda i, j: (i, 0)),
        out_shape=jax.ShapeDtypeStruct((t, d), F32),
        scratch_shapes=[pltpu.VMEM((d, tm), F32), pltpu.VMEM((te, tm), BF16)],
        compiler_params=_params("parallel", "arbitrary"),
        name="peer_dense",
    )(u, xt, vt, na, ea, r2, eb)


def _final_kernel(h_ref, po_ref, p_ref, g3_ref, wg_ref, wp_ref, y_ref):
    h = h_ref[...] + po_ref[...]
    n3 = _rms(h, g3_ref[...])
    gate = jax.nn.sigmoid(_dot(n3.astype(BF16), wg_ref[...]))
    y_ref[...] = h + gate * _dot(p_ref[...].astype(BF16), wp_ref[...])


def _final(h, po, p, g3, wg, wp):
    t, d = h.shape
    tm = min(t, 256)
    tok = lambda n: pl.BlockSpec((tm, n), lambda i: (i, 0))
    full = lambda a: pl.BlockSpec(a.shape, lambda i: (0,) * a.ndim)
    weights = [g3.reshape(1, -1), wg.astype(BF16), wp.astype(BF16)]
    return pl.pallas_call(
        _final_kernel,
        grid=(t // tm,),
        in_specs=[tok(d), tok(d), tok(p.shape[1])] + [full(a) for a in weights],
        out_specs=tok(d),
        out_shape=jax.ShapeDtypeStruct((t, d), F32),
        compiler_params=_params("parallel"),
        name="final",
    )(h, po, p, *weights)


def kernel(x_prompt, x_sample, cache_k, cache_v, p_prompt, p_sample, norm1_g, w_in, q_norm_g, k_norm_g, v_norm_g, spatial_w, spatial_b, out_norm_a_g, out_norm_b_g, w_out, norm2_g, peer_w_query, peer_sub_keys, peer_u, peer_v, norm3_g, ple_w_gate, ple_w_proj):
    depth = norm1_g.shape[0]
    bsz, s, d = x_prompt.shape
    db, dt, _ = x_sample.shape
    buf = cache_k.shape[2]
    assert bsz == 1 and dt == 1, "one prompt sequence and single-token decode rows"
    assert buf == MAX_WINDOW and s % (PATTERNS[-1][1] * BLK) == 0 and s >= MAX_WINDOW
    assert all(window == dil * BLK for window, dil in PATTERNS)
    pos_p = jnp.arange(s, dtype=jnp.int32)
    pos_s = jnp.full((db,), PAST_LEN, jnp.int32)
    hp, hs = x_prompt[0], x_sample[:, 0]
    outs = [[] for _ in range(5)]
    for l in range(depth):
        u_bf = peer_u[l].astype(BF16)
        vt_bf = peer_v[l].astype(BF16).T
        vg = v_norm_g[l].reshape(-1)

        def tail(h, atts, gm, p):
            h1, xt, st = _merge(h, atts, gm, out_norm_a_g[l], out_norm_b_g[l], w_out[l], norm2_g[l], peer_w_query[l], peer_sub_keys[l])
            po = _peer(xt, u_bf, vt_bf, *_select(st))
            return _final(h1, po, p, norm3_g[l], ple_w_gate[l], ple_w_proj[l])

        q, k, v, gm = _proj(hp, pos_p, norm1_g[l], w_in[l], q_norm_g[l], k_norm_g[l], vg, spatial_w[l], spatial_b[l], False)
        per_pat = [_attn_prompt(q, k, v, dil) for _, dil in PATTERNS]
        hp = tail(hp, [o for o, _ in per_pat] + [lse for _, lse in per_pat], gm, p_prompt[l, 0])
        keep = min(MAX_WINDOW, s)
        outs[0].append(k[s - keep :].reshape(1, keep, A_HEADS, HEAD_DIM))
        outs[1].append(v[s - keep :].reshape(1, keep, A_HEADS, HEAD_DIM))

        q, k, v, gm, vn = _proj(hs, pos_s, norm1_g[l], w_in[l], q_norm_g[l], k_norm_g[l], vg, spatial_w[l], spatial_b[l], True)
        att = _attn_sample(q, k, v, cache_k[l].reshape(db, buf, A_WIDTH), cache_v[l].reshape(db, buf, A_WIDTH))
        hs = tail(hs, [att], gm, p_sample[l, :, 0])
        for o, a in zip(outs[2:], (k, v, vn)):
            o.append(a.reshape(db, 1, A_HEADS, HEAD_DIM))
    return (hp[None], hs[:, None], *(jnp.stack(o) for o in outs))
```

```python
import functools

import jax
import jax.numpy as jnp
from jax import lax
from jax.experimental import pallas as pl
from jax.experimental.pallas import tpu as pltpu

HEAD_DIM = 64
A_HEADS = 8
A_WIDTH = A_HEADS * HEAD_DIM
B_GROUPS = 8
B_WIDTH = B_GROUPS * HEAD_DIM
PATTERNS = ((128, 1), (512, 4), (2048, 16))
MAX_WINDOW = 2048
PAST_LEN = 8192
BLK = 128
ROT_DIM = HEAD_DIM // 4
ROPE_THETA = 500000.0
PEER_HEADS = 8
PEER_NKEYS = 128
PEER_TOPK = 16
EPS = 1e-6
LANES = 128
BF16_SUBLANES = 16

VMEM_LIMIT = 48 * 1024 * 1024

F32 = jnp.float32
BF16 = jnp.bfloat16
NEG_INF = float("-inf")


def _params(*sem):
    return pltpu.CompilerParams(dimension_semantics=sem, vmem_limit_bytes=VMEM_LIMIT)


def _dot(a, b):
    return jnp.dot(a, b, preferred_element_type=F32)


def _dot_split(a, m, passes):
    acc = None
    rem = a
    for _ in range(passes):
        piece = rem.astype(BF16)
        rem = rem - piece.astype(F32)
        t = _dot(piece, m)
        acc = t if acc is None else acc + t
    return acc


def _rms(x, g):
    return x * lax.rsqrt(jnp.mean(x * x, axis=-1, keepdims=True) + EPS) * g


def _tile4(t):
    return jnp.concatenate([t, t, t, t], axis=1)


def _proj_kernel(x_ref, g1_ref, win_ref, qg_ref, kg_ref, vg_ref, c_ref, s1_ref, s2_ref, mh_ref, *rest, sample):
    if sample:
        w00_ref, b0_ref, q_out, k_out, v_out, gm_out, vn_out = rest
    else:
        ws_ref, bias_ref, q_out, k_out, v_out, gm_out = rest
    tm = x_ref.shape[0]
    n1 = _rms(x_ref[...], g1_ref[...])
    z = _dot(n1.astype(BF16), win_ref[...])
    mh = mh_ref[...]
    cos_t, sin_a, sin_b = _tile4(c_ref[...]), _tile4(s1_ref[...]), _tile4(s2_ref[...])

    def head_norm_rope(t, g):
        tn = t * lax.rsqrt(_dot_split(t * t, mh, 2) + EPS) * g
        return tn * cos_t + pltpu.roll(tn, A_WIDTH - ROT_DIM // 2, 1) * sin_a + pltpu.roll(tn, ROT_DIM // 2, 1) * sin_b

    q_out[...] = head_norm_rope(z[:, 0:A_WIDTH], qg_ref[...])
    k_out[...] = head_norm_rope(z[:, A_WIDTH : 2 * A_WIDTH], kg_ref[...])
    v_out[...] = z[:, 2 * A_WIDTH : 3 * A_WIDTH]
    u = jax.nn.gelu(z[:, 3 * A_WIDTH : 3 * A_WIDTH + B_WIDTH])
    gv = jax.nn.gelu(z[:, 3 * A_WIDTH + B_WIDTH :])
    vn = gv * lax.rsqrt(_dot_split(gv * gv, mh, 2) + EPS) * vg_ref[...]
    if sample:
        vn_out[...] = vn
        gm_out[...] = u * (vn * w00_ref[...] + b0_ref[...])
    else:
        row = lax.broadcasted_iota(jnp.int32, (BLK, BLK), 0)
        col = lax.broadcasted_iota(jnp.int32, (BLK, BLK), 1)
        lane_lo = col < HEAD_DIM
        causal = row >= col
        wm = [jnp.where(causal, ws_ref[g], 0.0).astype(BF16) for g in range(B_GROUPS)]
        vnb = vn.astype(BF16)
        for c in range(tm // BLK):
            pieces = []
            for j in range(B_WIDTH // LANES):
                s = vnb[c * BLK : (c + 1) * BLK, j * LANES : (j + 1) * LANES]
                pieces.append(jnp.where(lane_lo, _dot(wm[2 * j], s), _dot(wm[2 * j + 1], s)))
            mixed = jnp.concatenate(pieces, axis=1) + bias_ref[...]
            gm_out[c * BLK : (c + 1) * BLK, :] = u[c * BLK : (c + 1) * BLK, :] * mixed


def _proj(x, pos, g1, w_in, qg, kg, vg, ws, bs, sample):
    t, d = x.shape
    tm = min(t, 256)
    half = ROT_DIM // 2
    inv = ROPE_THETA ** (-jnp.arange(0, ROT_DIM, 2, dtype=F32) / ROT_DIM)
    ang = pos.astype(F32)[:, None] * inv[None, :]
    cos, sin = jnp.cos(ang), jnp.sin(ang)
    zeros = lambda n: jnp.zeros((t, n), F32)
    rest = HEAD_DIM - ROT_DIM
    c64 = jnp.concatenate([cos, cos, jnp.ones((t, rest), F32)], axis=1)
    sa64 = jnp.concatenate([-sin, zeros(half + rest)], axis=1)
    sb64 = jnp.concatenate([zeros(half), sin, zeros(rest)], axis=1)
    two = lambda a: jnp.concatenate([a, a], axis=1)
    head_id = jnp.arange(A_WIDTH) // HEAD_DIM
    mh = ((head_id[:, None] == head_id[None, :]).astype(F32) / HEAD_DIM).astype(BF16)
    row = lambda a: a.reshape(1, -1)
    tiled = lambda a: jnp.tile(a, A_HEADS).reshape(1, -1)
    full = lambda a: pl.BlockSpec(a.shape, lambda i: (0,) * a.ndim)
    tok = lambda n: pl.BlockSpec((tm, n), lambda i: (i, 0))
    args = [x, row(g1), w_in.astype(BF16), tiled(qg), tiled(kg), row(vg), two(c64), two(sa64), two(sb64), mh]
    specs = [tok(d)] + [full(a) for a in args[1:6]] + [tok(LANES)] * 3 + [full(mh)]
    if sample:
        extra = [jnp.repeat(ws[:, 0, 0], HEAD_DIM).reshape(1, -1), jnp.repeat(bs[:, 0], HEAD_DIM).reshape(1, -1)]
        n_out = 5
    else:
        extra = [ws, jnp.repeat(bs.T, HEAD_DIM, axis=1)]
        n_out = 4
    args += extra
    specs += [full(a) for a in extra]
    return pl.pallas_call(
        functools.partial(_proj_kernel, sample=sample),
        grid=(t // tm,),
        in_specs=specs,
        out_specs=[tok(A_WIDTH)] * n_out,
        out_shape=[jax.ShapeDtypeStruct((t, A_WIDTH), F32)] * n_out,
        compiler_params=_params("parallel"),
        name="proj_sample" if sample else "proj_prompt",
    )(*args)


def _attn_kernel(q_ref, k_ref, kp_ref, v_ref, vp_ref, o_ref, l_ref, *, nsub):
    first = pl.program_id(1) == 0
    qi = lax.broadcasted_iota(jnp.int32, (BLK, 2 * BLK), 0)
    kj = lax.broadcasted_iota(jnp.int32, (BLK, 2 * BLK), 1)
    dist = BLK + qi - kj
    band = (dist >= 0) & (dist <= BLK)
    band_first = band & ((kj >= BLK) | jnp.logical_not(first))
    lane_lo = lax.broadcasted_iota(jnp.int32, (BLK, LANES), 1) < HEAD_DIM
    scale = HEAD_DIM**-0.5
    for sb in range(nsub):
        r0 = sb * BLK
        rows = slice(r0, r0 + BLK)
        prev = slice(r0 - BLK, r0)
        valid = band_first if sb == 0 else band
        for j in range(A_WIDTH // LANES):
            cols = slice(j * LANES, (j + 1) * LANES)
            qf = q_ref[rows, cols]
            kprev = kp_ref[:, cols] if sb == 0 else k_ref[prev, cols]
            vprev = vp_ref[:, cols] if sb == 0 else v_ref[prev, cols]
            k2 = jnp.concatenate([kprev, k_ref[rows, cols]], axis=0).astype(BF16)
            v2 = jnp.concatenate([vprev, v_ref[rows, cols]], axis=0).astype(BF16)
            halves = []
            for lo in (True, False):
                qm = jnp.where(lane_lo == lo, qf, 0.0).astype(BF16)
                s = lax.dot_general(qm, k2, (((1,), (1,)), ((), ())), preferred_element_type=F32) * scale
                s = jnp.where(valid, s, NEG_INF)
                m = jnp.max(s, axis=1, keepdims=True)
                e = jnp.exp(s - m)
                den = jnp.sum(e, axis=1, keepdims=True)
                halves.append((_dot(e.astype(BF16), v2) / den, m + jnp.log(den)))
            o_ref[rows, cols] = jnp.where(lane_lo, halves[0][0], halves[1][0])
            l_ref[rows, cols] = jnp.where(lane_lo, halves[0][1], halves[1][1])


def _attn_prompt(q, k, v, dil):
    s = q.shape[0]
    length = s // dil
    qb = min(length, 512)
    nsub = qb // BLK
    view = lambda a: a.reshape(length, dil * A_WIDTH)
    own = pl.BlockSpec((qb, A_WIDTH), lambda r, b: (b, r))
    prev = pl.BlockSpec((BLK, A_WIDTH), lambda r, b: (jnp.maximum(b * nsub - 1, 0), r))
    o, lse = pl.pallas_call(
        functools.partial(_attn_kernel, nsub=nsub),
        grid=(dil, length // qb),
        in_specs=[own, own, prev, own, prev],
        out_specs=[own, own],
        out_shape=[jax.ShapeDtypeStruct((length, dil * A_WIDTH), F32)] * 2,
        compiler_params=_params("parallel", "arbitrary"),
        name=f"attn_d{dil}",
    )(view(q), view(k), view(k), view(v), view(v))
    return o.reshape(s, A_WIDTH), lse.reshape(s, A_WIDTH)


def _attn_sample_kernel(q_ref, ks_ref, vs_ref, k1_ref, k4_ref, k16_ref, v1_ref, v4_ref, v16_ref, o_ref):
    bb = q_ref.shape[0]
    scale = HEAD_DIM**-0.5
    n_pat = len(PATTERNS)

    def body(bi, carry):
        q = q_ref[bi] * scale
        s_self = jnp.sum(ks_ref[bi] * q, axis=-1, keepdims=True)
        s_pat = [jnp.sum(kr[bi] * q[None], axis=-1, keepdims=True) for kr in (k1_ref, k4_ref, k16_ref)]
        m = s_self
        for sp in s_pat:
            m = jnp.maximum(m, jnp.max(sp, axis=0))
        e_self = n_pat * jnp.exp(s_self - m)
        den = e_self
        acc = e_self * vs_ref[bi]
        for sp, vr in zip(s_pat, (v1_ref, v4_ref, v16_ref)):
            e = jnp.exp(sp - m[None])
            den = den + jnp.sum(e, axis=0)
            acc = acc + jnp.sum(e * vr[bi], axis=0)
        o_ref[bi] = acc / den
        return carry

    lax.fori_loop(0, bb, body, 0)


def _attn_sample(q, k, v, ck, cv):
    b, buf = ck.shape[0], ck.shape[1]
    bb = min(b, 4)
    views, specs = [], []
    for cache in (ck, cv):
        for window, dil in PATTERNS:
            nrow = buf // dil
            views.append(cache.reshape(b, nrow, dil, A_HEADS, HEAD_DIM))
            specs.append(pl.BlockSpec((bb, BLK, None, A_HEADS, HEAD_DIM), functools.partial(lambda i, nb: (i, nb, 0, 0, 0), nb=nrow // BLK - 1)))
    tok = pl.BlockSpec((bb, A_HEADS, HEAD_DIM), lambda i: (i, 0, 0))
    heads = lambda a: a.reshape(b, A_HEADS, HEAD_DIM)
    out = pl.pallas_call(
        _attn_sample_kernel,
        grid=(b // bb,),
        in_specs=[tok, tok, tok] + specs,
        out_specs=tok,
        out_shape=jax.ShapeDtypeStruct((b, A_HEADS, HEAD_DIM), F32),
        compiler_params=_params("parallel"),
        name="attn_sample",
    )(heads(q), heads(k), heads(v), *views)
    return out.reshape(b, A_WIDTH)


def _merge_kernel(*refs, n_pat):
    x_ref = refs[0]
    att_refs = refs[1 : 1 + 2 * n_pat] if n_pat else refs[1:2]
    gm_ref, ga_ref, gb_ref, wout_ref, g2_ref, wq_ref, sk_ref, h_out, xt_out, st_out = refs[-10:]
    if n_pat:
        outs = [r[...] for r in att_refs[:n_pat]]
        lses = [r[...] for r in att_refs[n_pat:]]
        m = functools.reduce(jnp.maximum, lses)
        ws = [jnp.exp(l - m) for l in lses]
        att = sum(w * o for w, o in zip(ws, outs)) / sum(ws)
    else:
        att = att_refs[0][...]
    cat = jnp.concatenate([_rms(att, ga_ref[...]), _rms(gm_ref[...], gb_ref[...])], axis=1)
    h = x_ref[...] + _dot(cat.astype(BF16), wout_ref[...])
    h_out[...] = h
    n2 = _rms(h, g2_ref[...])
    xt_out[...] = n2.T.astype(BF16)
    qt = _dot(n2.astype(BF16), wq_ref[...]).T.astype(BF16)
    for i in range(2 * PEER_HEADS):
        rows = slice(i * PEER_NKEYS, (i + 1) * PEER_NKEYS)
        st_out[rows, :] = _dot(sk_ref[i], qt[rows, :])


def _merge(x, atts, gm, ga, gb, w_out, g2, wq, sub_keys):
    t, d = x.shape
    tm = min(t, 256)
    n_pat = len(atts) // 2
    nq = wq.shape[1]
    sk = sub_keys.reshape(2 * PEER_HEADS, PEER_NKEYS, -1).astype(BF16)
    row = lambda a: a.reshape(1, -1)
    full = lambda a: pl.BlockSpec(a.shape, lambda i: (0,) * a.ndim)
    tok = lambda n: pl.BlockSpec((tm, n), lambda i: (i, 0))
    tokt = lambda n: pl.BlockSpec((n, tm), lambda i: (0, i))
    weights = [row(ga), row(gb), w_out.astype(BF16), row(g2), wq.astype(BF16), sk]
    return pl.pallas_call(
        functools.partial(_merge_kernel, n_pat=n_pat),
        grid=(t // tm,),
        in_specs=[tok(d)] + [tok(A_WIDTH)] * (len(atts) + 1) + [full(a) for a in weights],
        out_specs=[tok(d), tokt(d), tokt(nq)],
        out_shape=[
            jax.ShapeDtypeStruct((t, d), F32),
            jax.ShapeDtypeStruct((d, t), BF16),
            jax.ShapeDtypeStruct((nq, t), F32),
        ],
        compiler_params=_params("parallel"),
        name="merge_prompt" if n_pat else "merge_sample",
    )(x, *atts, gm, *weights)


def _top16(a, iota_k):
    rem = a
    rank = jnp.full(a.shape, PEER_TOPK, jnp.int32)
    vals = []
    for r in range(PEER_TOPK):
        m = jnp.max(rem, axis=0, keepdims=True)
        first = jnp.min(jnp.where(rem == m, iota_k, PEER_NKEYS), axis=0, keepdims=True)
        sel = iota_k == first
        rank = jnp.where(sel, r, rank)
        rem = jnp.where(sel, NEG_INF, rem)
        vals.append(m)
    return jnp.concatenate(vals, axis=0), rank


def _select_kernel(st_ref, na_ref, ea_ref, r2_ref, eb_ref):
    tk = st_ref.shape[1]
    iota_k = lax.broadcasted_iota(jnp.int32, (PEER_NKEYS, tk), 0)
    a = st_ref[0:PEER_NKEYS, :]
    b = st_ref[PEER_NKEYS:, :]
    ta, r1 = _top16(a, iota_k)
    tb, r2 = _top16(b, iota_k)
    cand = jnp.concatenate([ta[k : k + 1, :] + tb for k in range(PEER_TOPK)], axis=0)
    pos = lax.broadcasted_iota(jnp.int32, cand.shape, 0)
    rem = cand
    chosen = jnp.zeros(cand.shape, F32)
    for _ in range(PEER_TOPK):
        m = jnp.max(rem, axis=0, keepdims=True)
        first = jnp.min(jnp.where(rem == m, pos, cand.shape[0]), axis=0, keepdims=True)
        sel = pos == first
        chosen = jnp.where(sel, 1.0, chosen)
        rem = jnp.where(sel, NEG_INF, rem)
    z = jnp.sum(chosen * jnp.exp(cand - cand[0:1, :]), axis=0, keepdims=True)
    n_at = jnp.zeros(a.shape, F32)
    for k in range(PEER_TOPK):
        n_k = jnp.sum(chosen[k * PEER_TOPK : (k + 1) * PEER_TOPK, :], axis=0, keepdims=True)
        n_at = jnp.where(r1 == k, n_k, n_at)
    na_ref[0] = n_at
    ea_ref[0] = jnp.exp(a - ta[0:1, :]) / z
    r2_ref[0] = r2.astype(F32).astype(BF16)
    eb_ref[0] = jnp.exp(b - tb[0:1, :]).astype(BF16)


def _select(st):
    t = st.shape[1]
    tk = min(t, 256)
    out = pl.BlockSpec((1, PEER_NKEYS, tk), lambda i, h: (h, 0, i))
    return pl.pallas_call(
        _select_kernel,
        grid=(t // tk, PEER_HEADS),
        in_specs=[pl.BlockSpec((2 * PEER_NKEYS, tk), lambda i, h: (h, i))],
        out_specs=[out] * 4,
        out_shape=[jax.ShapeDtypeStruct((PEER_HEADS, PEER_NKEYS, t), dt) for dt in (F32, F32, BF16, BF16)],
        compiler_params=_params("parallel", "arbitrary"),
        name="peer_select",
    )(st)


def _peer_kernel(u_ref, xt_ref, vt_ref, na_ref, ea_ref, r2_ref, eb_ref, o_ref, acc_ref, gt_ref):
    j = pl.program_id(1)

    @pl.when(j == 0)
    def _():
        acc_ref[...] = jnp.zeros_like(acc_ref)

    tm = xt_ref.shape[1]
    sub = BF16_SUBLANES
    row_tile = lambda ref, h, ii: jnp.broadcast_to(ref[h, ii : ii + 1, :], (sub, tm)).astype(BF16)[None]
    tiles = lambda a: a.reshape(PEER_NKEYS // sub, sub, tm)
    st = _dot(u_ref[...], xt_ref[...])
    for ii in range(u_ref.shape[0] // PEER_NKEYS):
        w = None
        for h in range(PEER_HEADS):
            hit = tiles(r2_ref[h]) < row_tile(na_ref, h, ii)
            t = jnp.where(hit, tiles(eb_ref[h]), jnp.zeros((), BF16)) * row_tile(ea_ref, h, ii)
            w = t if w is None else w + t
        rows = slice(ii * PEER_NKEYS, (ii + 1) * PEER_NKEYS)
        g = tiles(jax.nn.gelu(st[rows, :]).astype(BF16)) * w
        gt_ref[rows, :] = g.reshape(PEER_NKEYS, tm)
    acc_ref[...] += _dot(vt_ref[...], gt_ref[...])

    @pl.when(j == pl.num_programs(1) - 1)
    def _():
        o_ref[...] = acc_ref[...].T


def _peer(xt, u, vt, na, ea, r2, eb):
    d, t = xt.shape
    e = u.shape[0]
    tm = min(t, 512)
    te = 1024
    i1_per = te // PEER_NKEYS
    sel_i1 = pl.BlockSpec((PEER_HEADS, i1_per, tm), lambda i, j: (0, j, i))
    sel_i2 = pl.BlockSpec((PEER_HEADS, PEER_NKEYS, tm), lambda i, j: (0, 0, i))
    return pl.pallas_call(
        _peer_kernel,
        grid=(t // tm, e // te),
        in_specs=[
            pl.BlockSpec((te, d), lambda i, j: (j, 0)),
            pl.BlockSpec((d, tm), lambda i, j: (0, i)),
            pl.BlockSpec((d, te), lambda i, j: (0, j)),
            sel_i1,
            sel_i1,
            sel_i2,
            sel_i2,
        ],
        out_specs=pl.BlockSpec((tm, d), lambda i, j: (i, 0)),
        out_shape=jax.ShapeDtypeStruct((t, d), F32),
        scratch_shapes=[pltpu.VMEM((d, tm), F32), pltpu.VMEM((te, tm), BF16)],
        compiler_params=_params("parallel", "arbitrary"),
        name="peer_dense",
    )(u, xt, vt, na, ea, r2, eb)


def _final_kernel(h_ref, po_ref, p_ref, g3_ref, wg_ref, wp_ref, y_ref):
    h = h_ref[...] + po_ref[...]
    n3 = _rms(h, g3_ref[...])
    gate = jax.nn.sigmoid(_dot(n3.astype(BF16), wg_ref[...]))
    y_ref[...] = h + gate * _dot(p_ref[...].astype(BF16), wp_ref[...])


def _final(h, po, p, g3, wg, wp):
    t, d = h.shape
    tm = min(t, 256)
    tok = lambda n: pl.BlockSpec((tm, n), lambda i: (i, 0))
    full = lambda a: pl.BlockSpec(a.shape, lambda i: (0,) * a.ndim)
    weights = [g3.reshape(1, -1), wg.astype(BF16), wp.astype(BF16)]
    return pl.pallas_call(
        _final_kernel,
        grid=(t // tm,),
        in_specs=[tok(d), tok(d), tok(p.shape[1])] + [full(a) for a in weights],
        out_specs=tok(d),
        out_shape=jax.ShapeDtypeStruct((t, d), F32),
        compiler_params=_params("parallel"),
        name="final",
    )(h, po, p, *weights)


def kernel(x_prompt, x_sample, cache_k, cache_v, p_prompt, p_sample, norm1_g, w_in, q_norm_g, k_norm_g, v_norm_g, spatial_w, spatial_b, out_norm_a_g, out_norm_b_g, w_out, norm2_g, peer_w_query, peer_sub_keys, peer_u, peer_v, norm3_g, ple_w_gate, ple_w_proj):
    depth = norm1_g.shape[0]
    bsz, s, d = x_prompt.shape
    db, dt, _ = x_sample.shape
    buf = cache_k.shape[2]
    assert bsz == 1 and dt == 1, "one prompt sequence and single-token decode rows"
    assert buf == MAX_WINDOW and s % (PATTERNS[-1][1] * BLK) == 0 and s >= MAX_WINDOW
    assert all(window == dil * BLK for window, dil in PATTERNS)
    pos_p = jnp.arange(s, dtype=jnp.int32)
    pos_s = jnp.full((db,), PAST_LEN, jnp.int32)
    hp, hs = x_prompt[0], x_sample[:, 0]
    outs = [[] for _ in range(5)]
    for l in range(depth):
        u_bf = peer_u[l].astype(BF16)
        vt_bf = peer_v[l].astype(BF16).T
        vg = v_norm_g[l].reshape(-1)

        def tail(h, atts, gm, p):
            h1, xt, st = _merge(h, atts, gm, out_norm_a_g[l], out_norm_b_g[l], w_out[l], norm2_g[l], peer_w_query[l], peer_sub_keys[l])
            po = _peer(xt, u_bf, vt_bf, *_select(st))
            return _final(h1, po, p, norm3_g[l], ple_w_gate[l], ple_w_proj[l])

        q, k, v, gm = _proj(hp, pos_p, norm1_g[l], w_in[l], q_norm_g[l], k_norm_g[l], vg, spatial_w[l], spatial_b[l], False)
        per_pat = [_attn_prompt(q, k, v, dil) for _, dil in PATTERNS]
        hp = tail(hp, [o for o, _ in per_pat] + [lse for _, lse in per_pat], gm, p_prompt[l, 0])
        keep = min(MAX_WINDOW, s)
        outs[0].append(k[s - keep :].reshape(1, keep, A_HEADS, HEAD_DIM))
        outs[1].append(v[s - keep :].reshape(1, keep, A_HEADS, HEAD_DIM))

        q, k, v, gm, vn = _proj(hs, pos_s, norm1_g[l], w_in[l], q_norm_g[l], k_norm_g[l], vg, spatial_w[l], spatial_b[l], True)
        att = _attn_sample(q, k, v, cache_k[l], cache_v[l])
        hs = tail(hs, [att], gm, p_sample[l, :, 0])
        for o, a in zip(outs[2:], (k, v, vn)):
            o.append(a.reshape(db, 1, A_HEADS, HEAD_DIM))
    return (hp[None], hs[:, None], *(jnp.stack(o) for o in outs))
```

```python
import functools

import jax
import jax.numpy as jnp
from jax import lax
from jax.experimental import pallas as pl
from jax.experimental.pallas import tpu as pltpu

HEAD_DIM = 64
A_HEADS = 8
A_WIDTH = A_HEADS * HEAD_DIM
B_GROUPS = 8
B_WIDTH = B_GROUPS * HEAD_DIM
PATTERNS = ((128, 1), (512, 4), (2048, 16))
MAX_WINDOW = 2048
PAST_LEN = 8192
BLK = 128
ROT_DIM = HEAD_DIM // 4
ROPE_THETA = 500000.0
PEER_HEADS = 8
PEER_NKEYS = 128
PEER_TOPK = 16
EPS = 1e-6
LANES = 128
BF16_SUBLANES = 16

VMEM_LIMIT = 48 * 1024 * 1024

F32 = jnp.float32
BF16 = jnp.bfloat16
NEG_INF = float("-inf")


def _params(*sem, flags=None):
    return pltpu.CompilerParams(dimension_semantics=sem, vmem_limit_bytes=VMEM_LIMIT, flags=flags)


def _dot(a, b):
    return jnp.dot(a, b, preferred_element_type=F32)


def _dot_split(a, m, passes):
    acc = None
    rem = a
    for _ in range(passes):
        piece = rem.astype(BF16)
        rem = rem - piece.astype(F32)
        t = _dot(piece, m)
        acc = t if acc is None else acc + t
    return acc


def _rms(x, g):
    return x * lax.rsqrt(jnp.mean(x * x, axis=-1, keepdims=True) + EPS) * g


def _tile4(t):
    return jnp.concatenate([t, t, t, t], axis=1)


def _proj_kernel(x_ref, g1_ref, win_ref, qg_ref, kg_ref, vg_ref, base_ref, cr_ref, sr_ref, ma_ref, mb_ref, mh_ref, *rest, sample):
    if sample:
        w00_ref, b0_ref, q_out, k_out, v_out, gm_out, vn_out = rest
    else:
        ws_ref, bias_ref, q_out, k_out, v_out, gm_out, *view_outs, slab_ref = rest
    tm = x_ref.shape[0]
    n1 = _rms(x_ref[...], g1_ref[...])
    z = _dot(n1.astype(BF16), win_ref[...])
    mh = mh_ref[...]
    cos_b, sin_b0 = base_ref[0, 0:1, :], base_ref[0, 1:2, :]
    cos_r, sin_r = cr_ref[...], sr_ref[...]
    cos_p = cos_b * cos_r - sin_b0 * sin_r
    sin_p = sin_b0 * cos_r + cos_b * sin_r
    cos_t, sin_a, sin_b = _tile4(cos_p), _tile4(sin_p * ma_ref[...]), _tile4(sin_p * mb_ref[...])

    def head_norm_rope(t, g):
        tn = t * lax.rsqrt(_dot_split(t * t, mh, 2) + EPS) * g
        return tn * cos_t + pltpu.roll(tn, A_WIDTH - ROT_DIM // 2, 1) * sin_a + pltpu.roll(tn, ROT_DIM // 2, 1) * sin_b

    q_out[...] = head_norm_rope(z[:, 0:A_WIDTH], qg_ref[...])
    k_out[...] = head_norm_rope(z[:, A_WIDTH : 2 * A_WIDTH], kg_ref[...])
    v_out[...] = z[:, 2 * A_WIDTH : 3 * A_WIDTH]
    u = jax.nn.gelu(z[:, 3 * A_WIDTH : 3 * A_WIDTH + B_WIDTH])
    gv = jax.nn.gelu(z[:, 3 * A_WIDTH + B_WIDTH :])
    vn = gv * lax.rsqrt(_dot_split(gv * gv, mh, 2) + EPS) * vg_ref[...]
    if sample:
        vn_out[...] = vn
        gm_out[...] = u * (vn * w00_ref[...] + b0_ref[...])
    else:
        row = lax.broadcasted_iota(jnp.int32, (BLK, BLK), 0)
        col = lax.broadcasted_iota(jnp.int32, (BLK, BLK), 1)
        lane_lo = col < HEAD_DIM
        causal = row >= col
        wm = [jnp.where(causal, ws_ref[g], 0.0).astype(BF16) for g in range(B_GROUPS)]
        vnb = vn.astype(BF16)
        for c in range(tm // BLK):
            pieces = []
            for j in range(B_WIDTH // LANES):
                s = vnb[c * BLK : (c + 1) * BLK, j * LANES : (j + 1) * LANES]
                pieces.append(jnp.where(lane_lo, _dot(wm[2 * j], s), _dot(wm[2 * j + 1], s)))
            mixed = jnp.concatenate(pieces, axis=1) + bias_ref[...]
            gm_out[c * BLK : (c + 1) * BLK, :] = u[c * BLK : (c + 1) * BLK, :] * mixed
        n_col = A_WIDTH // LANES
        for ai, src in enumerate((q_out, k_out, v_out)):
            for j in range(n_col):
                slab_ref[j] = src[:, j * LANES : (j + 1) * LANES]
            for vi, (_, dil) in enumerate(PATTERNS[1:]):
                dst = view_outs[3 * vi + ai]
                for r in range(dil):
                    for j in range(n_col):
                        c0 = r * A_WIDTH + j * LANES
                        dst[:, c0 : c0 + LANES] = slab_ref[j, pl.ds(r, tm // dil, stride=dil), :].astype(BF16)


def _proj(x, pos0, g1, w_in, qg, kg, vg, ws, bs, sample):
    t, d = x.shape
    tm = min(t, 256)
    nt = t // tm
    half = ROT_DIM // 2
    inv = ROPE_THETA ** (-jnp.arange(0, ROT_DIM, 2, dtype=F32) / ROT_DIM)
    lane = jnp.arange(LANES) % HEAD_DIM
    freq = jnp.where(lane < ROT_DIM, inv[lane % half], 0.0)
    ma = jnp.where(lane < half, -1.0, 0.0).reshape(1, -1)
    mb = jnp.where((lane >= half) & (lane < ROT_DIM), 1.0, 0.0).reshape(1, -1)
    step = 0 if sample else tm
    ang_base = (pos0 + step * jnp.arange(nt)).astype(F32)[:, None] * freq[None, :]
    base = jnp.zeros((nt, 8, LANES), F32).at[:, 0].set(jnp.cos(ang_base)).at[:, 1].set(jnp.sin(ang_base))
    ang_row = (jnp.zeros((tm,), F32) if sample else jnp.arange(tm, dtype=F32))[:, None] * freq[None, :]
    head_id = jnp.arange(A_WIDTH) // HEAD_DIM
    mh = ((head_id[:, None] == head_id[None, :]).astype(F32) / HEAD_DIM).astype(BF16)
    row = lambda a: a.reshape(1, -1)
    tiled = lambda a: jnp.tile(a, A_HEADS).reshape(1, -1)
    full = lambda a: pl.BlockSpec(a.shape, lambda i: (0,) * a.ndim)
    tok = lambda n: pl.BlockSpec((tm, n), lambda i: (i, 0))
    args = [x, row(g1), w_in.astype(BF16), tiled(qg), tiled(kg), row(vg), base, jnp.cos(ang_row), jnp.sin(ang_row), ma, mb, mh]
    specs = [tok(d)] + [full(a) for a in args[1:6]] + [pl.BlockSpec((1, 8, LANES), lambda i: (i, 0, 0))] + [full(a) for a in args[7:]]
    if sample:
        extra = [jnp.repeat(ws[:, 0, 0], HEAD_DIM).reshape(1, -1), jnp.repeat(bs[:, 0], HEAD_DIM).reshape(1, -1)]
        out_specs = [tok(A_WIDTH)] * 5
        out_shape = [jax.ShapeDtypeStruct((t, A_WIDTH), F32)] * 5
    else:
        extra = [ws, jnp.repeat(bs.T, HEAD_DIM, axis=1)]
        out_specs = [tok(A_WIDTH)] * 4
        out_shape = [jax.ShapeDtypeStruct((t, A_WIDTH), F32)] * 4
        for _, dil in PATTERNS[1:]:
            out_specs += [pl.BlockSpec((tm // dil, dil * A_WIDTH), lambda i: (i, 0))] * 3
            out_shape += [jax.ShapeDtypeStruct((t // dil, dil * A_WIDTH), BF16)] * 3
    args += extra
    specs += [full(a) for a in extra]
    return pl.pallas_call(
        functools.partial(_proj_kernel, sample=sample),
        grid=(nt,),
        in_specs=specs,
        out_specs=out_specs,
        out_shape=out_shape,
        scratch_shapes=[] if sample else [pltpu.VMEM((A_WIDTH // LANES, tm, LANES), F32)],
        compiler_params=_params("parallel"),
        name="proj_sample" if sample else "proj_prompt",
    )(*args)


def _attn_kernel(q_ref, k_ref, kp_ref, v_ref, vp_ref, o_ref, l_ref, *, nsub):
    first = pl.program_id(1) == 0
    qi = lax.broadcasted_iota(jnp.int32, (BLK, 2 * BLK), 0)
    kj = lax.broadcasted_iota(jnp.int32, (BLK, 2 * BLK), 1)
    dist = BLK + qi - kj
    band = (dist >= 0) & (dist <= BLK)
    band_first = band & ((kj >= BLK) | jnp.logical_not(first))
    lane_lo = lax.broadcasted_iota(jnp.int32, (BLK, LANES), 1) < HEAD_DIM
    scale = HEAD_DIM**-0.5
    for sb in range(nsub):
        r0 = sb * BLK
        rows = slice(r0, r0 + BLK)
        prev = slice(r0 - BLK, r0)
        valid = band_first if sb == 0 else band
        for j in range(A_WIDTH // LANES):
            cols = slice(j * LANES, (j + 1) * LANES)
            qf = q_ref[rows, cols]
            kprev = kp_ref[:, cols] if sb == 0 else k_ref[prev, cols]
            vprev = vp_ref[:, cols] if sb == 0 else v_ref[prev, cols]
            k2 = jnp.concatenate([kprev, k_ref[rows, cols]], axis=0).astype(BF16)
            v2 = jnp.concatenate([vprev, v_ref[rows, cols]], axis=0).astype(BF16)
            halves = []
            for lo in (True, False):
                qm = jnp.where(lane_lo == lo, qf, jnp.zeros((), qf.dtype)).astype(BF16)
                s = lax.dot_general(qm, k2, (((1,), (1,)), ((), ())), preferred_element_type=F32) * scale
                s = jnp.where(valid, s, NEG_INF)
                m = jnp.max(s, axis=1, keepdims=True)
                e = jnp.exp(s - m)
                den = jnp.sum(e, axis=1, keepdims=True)
                halves.append((_dot(e.astype(BF16), v2) / den, m + jnp.log(den)))
            o_ref[rows, cols] = jnp.where(lane_lo, halves[0][0], halves[1][0])
            l_ref[rows, cols] = jnp.where(lane_lo, halves[0][1], halves[1][1])


def _attn_prompt(qv, kv, vv, dil):
    length = qv.shape[0]
    qb = min(length, 512)
    nsub = qb // BLK
    own = pl.BlockSpec((qb, A_WIDTH), lambda r, b: (b, r))
    prev = pl.BlockSpec((BLK, A_WIDTH), lambda r, b: (jnp.maximum(b * nsub - 1, 0), r))
    return pl.pallas_call(
        functools.partial(_attn_kernel, nsub=nsub),
        grid=(dil, length // qb),
        in_specs=[own, own, prev, own, prev],
        out_specs=[own, own],
        out_shape=[jax.ShapeDtypeStruct((length, dil * A_WIDTH), F32)] * 2,
        compiler_params=_params("parallel", "arbitrary"),
        name=f"attn_d{dil}",
    )(qv, kv, kv, vv, vv)


def _attn_sample_kernel(q_ref, ks_ref, vs_ref, k_ref, v_ref, w_ref, o_ref):
    scale = HEAD_DIM**-0.5
    w = w_ref[...]
    live = w > 0.0
    for h in range(A_HEADS):
        q = q_ref[0, h] * scale
        s = jnp.where(live, jnp.sum(k_ref[0, h] * q, axis=0, keepdims=True), NEG_INF)
        s_self = jnp.sum(ks_ref[0, h] * q, axis=0, keepdims=True)
        m = jnp.maximum(jnp.max(s, axis=1, keepdims=True), s_self)
        e = w * jnp.exp(s - m)
        e_self = len(PATTERNS) * jnp.exp(s_self - m)
        den = jnp.sum(e, axis=1, keepdims=True) + e_self
        acc = jnp.sum(v_ref[0, h] * e, axis=1, keepdims=True) + e_self * vs_ref[0, h]
        o_ref[0, h] = acc / den


def _attn_sample(q, k, v, ck, cv):
    b, buf = ck.shape[0], ck.shape[1]
    mult = [0.0] * buf
    for window, dil in PATTERNS:
        for j in range(1, window // dil + 1):
            if buf - dil * j >= 0:
                mult[buf - dil * j] += 1.0
    w = jnp.asarray(mult, F32).reshape(1, buf)
    tok = pl.BlockSpec((1, A_HEADS, HEAD_DIM, 1), lambda i: (i, 0, 0, 0))
    cache = pl.BlockSpec((1, A_HEADS, HEAD_DIM, buf), lambda i: (i, 0, 0, 0))
    col = lambda a: a.reshape(b, A_HEADS, HEAD_DIM, 1)
    pos_minor = lambda a: jnp.transpose(a, (0, 2, 3, 1))
    out = pl.pallas_call(
        _attn_sample_kernel,
        grid=(b,),
        in_specs=[tok, tok, tok, cache, cache, pl.BlockSpec((1, buf), lambda i: (0, 0))],
        out_specs=tok,
        out_shape=jax.ShapeDtypeStruct((b, A_HEADS, HEAD_DIM, 1), F32),
        compiler_params=_params("parallel"),
        name="attn_sample",
    )(col(q), col(k), col(v), pos_minor(ck), pos_minor(cv), w)
    return out.reshape(b, A_WIDTH)


def _merge_kernel(*refs, n_pat):
    x_ref = refs[0]
    att_refs = refs[1 : 1 + 2 * n_pat] if n_pat else refs[1:2]
    gm_ref, ga_ref, gb_ref, wout_ref, g2_ref, wq_ref, sk_ref, h_out, xt_out, st_out = refs[1 + len(att_refs) :][:10]
    scratch = refs[11 + len(att_refs) :]
    tm = x_ref.shape[0]
    if n_pat:
        vals = []
        for idx, ref in enumerate(att_refs):
            dil = PATTERNS[idx % n_pat][1]
            if dil == 1:
                vals.append(ref[...])
                continue
            nat = scratch[idx]
            n_col = A_WIDTH // LANES
            for r in range(dil):
                for j in range(n_col):
                    c0 = r * A_WIDTH + j * LANES
                    nat[j, pl.ds(r, tm // dil, stride=dil), :] = ref[:, c0 : c0 + LANES]
            vals.append(jnp.concatenate([nat[j] for j in range(n_col)], axis=1))
        outs, lses = vals[:n_pat], vals[n_pat:]
        m = functools.reduce(jnp.maximum, lses)
        ws = [jnp.exp(l - m) for l in lses]
        att = sum(w * o for w, o in zip(ws, outs)) / sum(ws)
    else:
        att = att_refs[0][...]
    cat = jnp.concatenate([_rms(att, ga_ref[...]), _rms(gm_ref[...], gb_ref[...])], axis=1)
    h = x_ref[...] + _dot(cat.astype(BF16), wout_ref[...])
    h_out[...] = h
    n2 = _rms(h, g2_ref[...])
    xt_out[...] = n2.T.astype(BF16)
    qt = _dot(n2.astype(BF16), wq_ref[...]).T.astype(BF16)
    for i in range(2 * PEER_HEADS):
        rows = slice(i * PEER_NKEYS, (i + 1) * PEER_NKEYS)
        st_out[rows, :] = _dot(sk_ref[i], qt[rows, :])


def _merge(x, atts, gm, ga, gb, w_out, g2, wq, sub_keys):
    t, d = x.shape
    tm = min(t, 256)
    n_pat = len(atts) // 2
    nq = wq.shape[1]
    sk = sub_keys.reshape(2 * PEER_HEADS, PEER_NKEYS, -1).astype(BF16)
    row = lambda a: a.reshape(1, -1)
    full = lambda a: pl.BlockSpec(a.shape, lambda i: (0,) * a.ndim)
    tok = lambda n: pl.BlockSpec((tm, n), lambda i: (i, 0))
    tokt = lambda n: pl.BlockSpec((n, tm), lambda i: (0, i))
    view = lambda a: pl.BlockSpec((tm * A_WIDTH // a.shape[1], a.shape[1]), lambda i: (i, 0))
    weights = [row(ga), row(gb), w_out.astype(BF16), row(g2), wq.astype(BF16), sk]
    return pl.pallas_call(
        functools.partial(_merge_kernel, n_pat=n_pat),
        grid=(t // tm,),
        in_specs=[tok(d)] + [view(a) for a in atts] + [tok(A_WIDTH)] + [full(a) for a in weights],
        out_specs=[tok(d), tokt(d), tokt(nq)],
        out_shape=[
            jax.ShapeDtypeStruct((t, d), F32),
            jax.ShapeDtypeStruct((d, t), BF16),
            jax.ShapeDtypeStruct((nq, t), F32),
        ],
        scratch_shapes=[pltpu.VMEM((A_WIDTH // LANES, tm, LANES), F32)] * (2 * n_pat),
        compiler_params=_params("parallel"),
        name="merge_prompt" if n_pat else "merge_sample",
    )(x, *atts, gm, *weights)


def _top16(a, iota_k):
    rem = a
    rank = jnp.full(a.shape, PEER_TOPK, jnp.int32)
    vals = []
    for r in range(PEER_TOPK):
        m = jnp.max(rem, axis=0, keepdims=True)
        first = jnp.min(jnp.where(rem == m, iota_k, PEER_NKEYS), axis=0, keepdims=True)
        sel = iota_k == first
        rank = jnp.where(sel, r, rank)
        rem = jnp.where(sel, NEG_INF, rem)
        vals.append(m)
    return jnp.concatenate(vals, axis=0), rank


def _select_kernel(st_ref, na_ref, ea_ref, r2_ref, eb_ref):
    tk = st_ref.shape[1]
    iota_k = lax.broadcasted_iota(jnp.int32, (PEER_NKEYS, tk), 0)
    a = st_ref[0:PEER_NKEYS, :]
    b = st_ref[PEER_NKEYS:, :]
    ta, r1 = _top16(a, iota_k)
    tb, r2 = _top16(b, iota_k)
    cand = jnp.concatenate([ta[k : k + 1, :] + tb for k in range(PEER_TOPK)], axis=0)
    pos = lax.broadcasted_iota(jnp.int32, cand.shape, 0)
    rem = cand
    chosen = jnp.zeros(cand.shape, F32)
    for _ in range(PEER_TOPK):
        m = jnp.max(rem, axis=0, keepdims=True)
        first = jnp.min(jnp.where(rem == m, pos, cand.shape[0]), axis=0, keepdims=True)
        sel = pos == first
        chosen = jnp.where(sel, 1.0, chosen)
        rem = jnp.where(sel, NEG_INF, rem)
    z = jnp.sum(chosen * jnp.exp(cand - cand[0:1, :]), axis=0, keepdims=True)
    n_at = jnp.zeros(a.shape, F32)
    for k in range(PEER_TOPK):
        n_k = jnp.sum(chosen[k * PEER_TOPK : (k + 1) * PEER_TOPK, :], axis=0, keepdims=True)
        n_at = jnp.where(r1 == k, n_k, n_at)
    na_ref[0] = n_at
    ea_ref[0] = jnp.exp(a - ta[0:1, :]) / z
    r2_ref[0] = r2.astype(F32).astype(BF16)
    eb_ref[0] = jnp.exp(b - tb[0:1, :]).astype(BF16)


def _select(st):
    t = st.shape[1]
    tk = min(t, 256)
    out = pl.BlockSpec((1, PEER_NKEYS, tk), lambda i, h: (h, 0, i))
    return pl.pallas_call(
        _select_kernel,
        grid=(t // tk, PEER_HEADS),
        in_specs=[pl.BlockSpec((2 * PEER_NKEYS, tk), lambda i, h: (h, i))],
        out_specs=[out] * 4,
        out_shape=[jax.ShapeDtypeStruct((PEER_HEADS, PEER_NKEYS, t), dt) for dt in (F32, F32, BF16, BF16)],
        compiler_params=_params("parallel", "arbitrary"),
        name="peer_select",
    )(st)


def _peer_kernel(u_ref, xt_ref, vt_ref, na_ref, ea_ref, r2_ref, eb_ref, o_ref, acc_ref, gt_ref):
    j = pl.program_id(1)

    @pl.when(j == 0)
    def _():
        acc_ref[...] = jnp.zeros_like(acc_ref)

    tm = xt_ref.shape[1]
    sub = BF16_SUBLANES
    row_tile = lambda ref, h, ii: jnp.broadcast_to(ref[h, ii : ii + 1, :], (sub, tm)).astype(BF16)[None]
    tiles = lambda a: a.reshape(PEER_NKEYS // sub, sub, tm)
    st = _dot(u_ref[...], xt_ref[...])
    for ii in range(u_ref.shape[0] // PEER_NKEYS):
        w = None
        for h in range(PEER_HEADS):
            hit = tiles(r2_ref[h]) < row_tile(na_ref, h, ii)
            t = jnp.where(hit, tiles(eb_ref[h]), jnp.zeros((), BF16)) * row_tile(ea_ref, h, ii)
            w = t if w is None else w + t
        rows = slice(ii * PEER_NKEYS, (ii + 1) * PEER_NKEYS)
        g = tiles(jax.nn.gelu(st[rows, :]).astype(BF16)) * w
        gt_ref[rows, :] = g.reshape(PEER_NKEYS, tm)
    acc_ref[...] += _dot(vt_ref[...], gt_ref[...])

    @pl.when(j == pl.num_programs(1) - 1)
    def _():
        o_ref[...] = acc_ref[...].T


def _peer(xt, u, vt, na, ea, r2, eb):
    d, t = xt.shape
    e = u.shape[0]
    tm = min(t, 512)
    te = 1024
    i1_per = te // PEER_NKEYS
    sel_i1 = pl.BlockSpec((PEER_HEADS, i1_per, tm), lambda i, j: (0, j, i))
    sel_i2 = pl.BlockSpec((PEER_HEADS, PEER_NKEYS, tm), lambda i, j: (0, 0, i))
    return pl.pallas_call(
        _peer_kernel,
        grid=(t // tm, e // te),
        in_specs=[
            pl.BlockSpec((te, d), lambda i, j: (j, 0)),
            pl.BlockSpec((d, tm), lambda i, j: (0, i)),
            pl.BlockSpec((d, te), lambda i, j: (0, j)),
            sel_i1,
            sel_i1,
            sel_i2,
            sel_i2,
        ],
        out_specs=pl.BlockSpec((tm, d), lambda i, j: (i, 0)),
        out_shape=jax.ShapeDtypeStruct((t, d), F32),
        scratch_shapes=[pltpu.VMEM((d, tm), F32), pltpu.VMEM((te, tm), BF16)],
        compiler_params=_params("parallel", "arbitrary"),
        name="peer_dense",
    )(u, xt, vt, na, ea, r2, eb)


def _final_kernel(h_ref, po_ref, p_ref, g3_ref, wg_ref, wp_ref, y_ref):
    h = h_ref[...] + po_ref[...]
    n3 = _rms(h, g3_ref[...])
    gate = jax.nn.sigmoid(_dot(n3.astype(BF16), wg_ref[...]))
    y_ref[...] = h + gate * _dot(p_ref[...].astype(BF16), wp_ref[...])


def _final(h, po, p, g3, wg, wp):
    t, d = h.shape
    tm = min(t, 256)
    tok = lambda n: pl.BlockSpec((tm, n), lambda i: (i, 0))
    full = lambda a: pl.BlockSpec(a.shape, lambda i: (0,) * a.ndim)
    weights = [g3.reshape(1, -1), wg.astype(BF16), wp.astype(BF16)]
    return pl.pallas_call(
        _final_kernel,
        grid=(t // tm,),
        in_specs=[tok(d), tok(d), tok(p.shape[1])] + [full(a) for a in weights],
        out_specs=tok(d),
        out_shape=jax.ShapeDtypeStruct((t, d), F32),
        compiler_params=_params("parallel"),
        name="final",
    )(h, po, p, *weights)


def kernel(x_prompt, x_sample, cache_k, cache_v, p_prompt, p_sample, norm1_g, w_in, q_norm_g, k_norm_g, v_norm_g, spatial_w, spatial_b, out_norm_a_g, out_norm_b_g, w_out, norm2_g, peer_w_query, peer_sub_keys, peer_u, peer_v, norm3_g, ple_w_gate, ple_w_proj):
    depth = norm1_g.shape[0]
    bsz, s, d = x_prompt.shape
    db, dt, _ = x_sample.shape
    buf = cache_k.shape[2]
    assert bsz == 1 and dt == 1, "one prompt sequence and single-token decode rows"
    assert buf == MAX_WINDOW and s % (PATTERNS[-1][1] * BLK) == 0 and s >= MAX_WINDOW
    assert all(window == dil * BLK for window, dil in PATTERNS)
    assert PATTERNS[0][1] == 1
    hp, hs = x_prompt[0], x_sample[:, 0]
    outs = [[] for _ in range(5)]
    for l in range(depth):
        u_bf = peer_u[l].astype(BF16)
        vt_bf = peer_v[l].astype(BF16).T
        vg = v_norm_g[l].reshape(-1)

        def tail(h, atts, gm, p):
            h1, xt, st = _merge(h, atts, gm, out_norm_a_g[l], out_norm_b_g[l], w_out[l], norm2_g[l], peer_w_query[l], peer_sub_keys[l])
            po = _peer(xt, u_bf, vt_bf, *_select(st))
            return _final(h1, po, p, norm3_g[l], ple_w_gate[l], ple_w_proj[l])

        q, k, v, gm, *views = _proj(hp, 0, norm1_g[l], w_in[l], q_norm_g[l], k_norm_g[l], vg, spatial_w[l], spatial_b[l], False)
        per_pat = [_attn_prompt(q, k, v, 1)] + [_attn_prompt(*views[3 * i : 3 * i + 3], dil) for i, (_, dil) in enumerate(PATTERNS[1:])]
        hp = tail(hp, [o for o, _ in per_pat] + [lse for _, lse in per_pat], gm, p_prompt[l, 0])
        keep = min(MAX_WINDOW, s)
        outs[0].append(k[s - keep :].reshape(1, keep, A_HEADS, HEAD_DIM))
        outs[1].append(v[s - keep :].reshape(1, keep, A_HEADS, HEAD_DIM))

        q, k, v, gm, vn = _proj(hs, PAST_LEN, norm1_g[l], w_in[l], q_norm_g[l], k_norm_g[l], vg, spatial_w[l], spatial_b[l], True)
        att = _attn_sample(q, k, v, cache_k[l], cache_v[l])
        hs = tail(hs, [att], gm, p_sample[l, :, 0])
        for o, a in zip(outs[2:], (k, v, vn)):
            o.append(a.reshape(db, 1, A_HEADS, HEAD_DIM))
    return (hp[None], hs[:, None], *(jnp.stack(o) for o in outs))
```

```python
import functools

import jax
import jax.numpy as jnp
from jax import lax
from jax.experimental import pallas as pl
from jax.experimental.pallas import tpu as pltpu

HEAD_DIM = 64
A_HEADS = 8
A_WIDTH = A_HEADS * HEAD_DIM
B_GROUPS = 8
B_WIDTH = B_GROUPS * HEAD_DIM
PATTERNS = ((128, 1), (512, 4), (2048, 16))
MAX_WINDOW = 2048
PAST_LEN = 8192
BLK = 128
ROT_DIM = HEAD_DIM // 4
ROPE_THETA = 500000.0
PEER_HEADS = 8
PEER_NKEYS = 128
PEER_TOPK = 16
EPS = 1e-6
LANES = 128
BF16_SUBLANES = 16

VMEM_LIMIT = 48 * 1024 * 1024

F32 = jnp.float32
BF16 = jnp.bfloat16
NEG_INF = float("-inf")


def _params(*sem, flags=None):
    return pltpu.CompilerParams(dimension_semantics=sem, vmem_limit_bytes=VMEM_LIMIT, flags=flags)


def _dot(a, b):
    return jnp.dot(a, b, preferred_element_type=F32)


def _dot_split(a, m, passes):
    acc = None
    rem = a
    for _ in range(passes):
        piece = rem.astype(BF16)
        rem = rem - piece.astype(F32)
        t = _dot(piece, m)
        acc = t if acc is None else acc + t
    return acc


def _rms(x, g):
    return x * lax.rsqrt(jnp.mean(x * x, axis=-1, keepdims=True) + EPS) * g


def _tile4(t):
    return jnp.concatenate([t, t, t, t], axis=1)


def _proj_kernel(x_ref, g1_ref, win_ref, qg_ref, kg_ref, vg_ref, base_ref, cr_ref, sr_ref, ma_ref, mb_ref, mh_ref, *rest, sample):
    if sample:
        w00_ref, b0_ref, q_out, k_out, v_out, gm_out, vn_out = rest
    else:
        ws_ref, bias_ref, q_out, k_out, v_out, gm_out, *view_outs, slab_ref = rest
    tm = x_ref.shape[0]
    n1 = _rms(x_ref[...], g1_ref[...])
    z = _dot(n1.astype(BF16), win_ref[...])
    mh = mh_ref[...]
    cos_b, sin_b0 = base_ref[0, 0:1, :], base_ref[0, 1:2, :]
    cos_r, sin_r = cr_ref[...], sr_ref[...]
    cos_p = cos_b * cos_r - sin_b0 * sin_r
    sin_p = sin_b0 * cos_r + cos_b * sin_r
    cos_t, sin_a, sin_b = _tile4(cos_p), _tile4(sin_p * ma_ref[...]), _tile4(sin_p * mb_ref[...])

    def head_norm_rope(t, g):
        tn = t * lax.rsqrt(_dot_split(t * t, mh, 2) + EPS) * g
        return tn * cos_t + pltpu.roll(tn, A_WIDTH - ROT_DIM // 2, 1) * sin_a + pltpu.roll(tn, ROT_DIM // 2, 1) * sin_b

    q_out[...] = head_norm_rope(z[:, 0:A_WIDTH], qg_ref[...])
    k_out[...] = head_norm_rope(z[:, A_WIDTH : 2 * A_WIDTH], kg_ref[...])
    v_out[...] = z[:, 2 * A_WIDTH : 3 * A_WIDTH]
    u = jax.nn.gelu(z[:, 3 * A_WIDTH : 3 * A_WIDTH + B_WIDTH])
    gv = jax.nn.gelu(z[:, 3 * A_WIDTH + B_WIDTH :])
    vn = gv * lax.rsqrt(_dot_split(gv * gv, mh, 2) + EPS) * vg_ref[...]
    if sample:
        vn_out[...] = vn
        gm_out[...] = u * (vn * w00_ref[...] + b0_ref[...])
    else:
        row = lax.broadcasted_iota(jnp.int32, (BLK, BLK), 0)
        col = lax.broadcasted_iota(jnp.int32, (BLK, BLK), 1)
        lane_lo = col < HEAD_DIM
        causal = row >= col
        wm = [jnp.where(causal, ws_ref[g], 0.0).astype(BF16) for g in range(B_GROUPS)]
        vnb = vn.astype(BF16)
        for c in range(tm // BLK):
            pieces = []
            for j in range(B_WIDTH // LANES):
                s = vnb[c * BLK : (c + 1) * BLK, j * LANES : (j + 1) * LANES]
                pieces.append(jnp.where(lane_lo, _dot(wm[2 * j], s), _dot(wm[2 * j + 1], s)))
            mixed = jnp.concatenate(pieces, axis=1) + bias_ref[...]
            gm_out[c * BLK : (c + 1) * BLK, :] = u[c * BLK : (c + 1) * BLK, :] * mixed
        n_col = A_WIDTH // LANES
        for ai, src in enumerate((q_out, k_out, v_out)):
            for j in range(n_col):
                slab_ref[j] = src[:, j * LANES : (j + 1) * LANES]
            for vi, (_, dil) in enumerate(PATTERNS[1:]):
                dst = view_outs[3 * vi + ai]
                for r in range(dil):
                    for j in range(n_col):
                        c0 = r * A_WIDTH + j * LANES
                        dst[:, c0 : c0 + LANES] = slab_ref[j, pl.ds(r, tm // dil, stride=dil), :].astype(BF16)


def _proj(x, pos0, g1, w_in, qg, kg, vg, ws, bs, sample):
    t, d = x.shape
    tm = min(t, 256)
    nt = t // tm
    half = ROT_DIM // 2
    inv = ROPE_THETA ** (-jnp.arange(0, ROT_DIM, 2, dtype=F32) / ROT_DIM)
    lane = jnp.arange(LANES) % HEAD_DIM
    freq = jnp.where(lane < ROT_DIM, inv[lane % half], 0.0)
    ma = jnp.where(lane < half, -1.0, 0.0).reshape(1, -1)
    mb = jnp.where((lane >= half) & (lane < ROT_DIM), 1.0, 0.0).reshape(1, -1)
    step = 0 if sample else tm
    ang_base = (pos0 + step * jnp.arange(nt)).astype(F32)[:, None] * freq[None, :]
    base = jnp.zeros((nt, 8, LANES), F32).at[:, 0].set(jnp.cos(ang_base)).at[:, 1].set(jnp.sin(ang_base))
    ang_row = (jnp.zeros((tm,), F32) if sample else jnp.arange(tm, dtype=F32))[:, None] * freq[None, :]
    head_id = jnp.arange(A_WIDTH) // HEAD_DIM
    mh = ((head_id[:, None] == head_id[None, :]).astype(F32) / HEAD_DIM).astype(BF16)
    row = lambda a: a.reshape(1, -1)
    tiled = lambda a: jnp.tile(a, A_HEADS).reshape(1, -1)
    full = lambda a: pl.BlockSpec(a.shape, lambda i: (0,) * a.ndim)
    tok = lambda n: pl.BlockSpec((tm, n), lambda i: (i, 0))
    args = [x, row(g1), w_in.astype(BF16), tiled(qg), tiled(kg), row(vg), base, jnp.cos(ang_row), jnp.sin(ang_row), ma, mb, mh]
    specs = [tok(d)] + [full(a) for a in args[1:6]] + [pl.BlockSpec((1, 8, LANES), lambda i: (i, 0, 0))] + [full(a) for a in args[7:]]
    if sample:
        extra = [jnp.repeat(ws[:, 0, 0], HEAD_DIM).reshape(1, -1), jnp.repeat(bs[:, 0], HEAD_DIM).reshape(1, -1)]
        out_specs = [tok(A_WIDTH)] * 5
        out_shape = [jax.ShapeDtypeStruct((t, A_WIDTH), F32)] * 5
    else:
        extra = [ws, jnp.repeat(bs.T, HEAD_DIM, axis=1)]
        out_specs = [tok(A_WIDTH)] * 4
        out_shape = [jax.ShapeDtypeStruct((t, A_WIDTH), F32)] * 4
        for _, dil in PATTERNS[1:]:
            out_specs += [pl.BlockSpec((tm // dil, dil * A_WIDTH), lambda i: (i, 0))] * 3
            out_shape += [jax.ShapeDtypeStruct((t // dil, dil * A_WIDTH), BF16)] * 3
    args += extra
    specs += [full(a) for a in extra]
    return pl.pallas_call(
        functools.partial(_proj_kernel, sample=sample),
        grid=(nt,),
        in_specs=specs,
        out_specs=out_specs,
        out_shape=out_shape,
        scratch_shapes=[] if sample else [pltpu.VMEM((A_WIDTH // LANES, tm, LANES), F32)],
        compiler_params=_params("parallel"),
        name="proj_sample" if sample else "proj_prompt",
    )(*args)


def _attn_kernel(q_ref, k_ref, kp_ref, v_ref, vp_ref, o_ref, l_ref, *, nsub):
    first = pl.program_id(1) == 0
    qi = lax.broadcasted_iota(jnp.int32, (BLK, 2 * BLK), 0)
    kj = lax.broadcasted_iota(jnp.int32, (BLK, 2 * BLK), 1)
    dist = BLK + qi - kj
    band = (dist >= 0) & (dist <= BLK)
    band_first = band & ((kj >= BLK) | jnp.logical_not(first))
    lane_lo = lax.broadcasted_iota(jnp.int32, (BLK, LANES), 1) < HEAD_DIM
    scale = HEAD_DIM**-0.5
    for sb in range(nsub):
        r0 = sb * BLK
        rows = slice(r0, r0 + BLK)
        prev = slice(r0 - BLK, r0)
        valid = band_first if sb == 0 else band
        for j in range(A_WIDTH // LANES):
            cols = slice(j * LANES, (j + 1) * LANES)
            qf = q_ref[rows, cols]
            kprev = kp_ref[:, cols] if sb == 0 else k_ref[prev, cols]
            vprev = vp_ref[:, cols] if sb == 0 else v_ref[prev, cols]
            k2 = jnp.concatenate([kprev, k_ref[rows, cols]], axis=0).astype(BF16)
            v2 = jnp.concatenate([vprev, v_ref[rows, cols]], axis=0).astype(BF16)
            halves = []
            for lo in (True, False):
                qm = jnp.where(lane_lo == lo, qf, jnp.zeros((), qf.dtype)).astype(BF16)
                s = lax.dot_general(qm, k2, (((1,), (1,)), ((), ())), preferred_element_type=F32) * scale
                s = jnp.where(valid, s, NEG_INF)
                m = jnp.max(s, axis=1, keepdims=True)
                e = jnp.exp(s - m)
                den = jnp.sum(e, axis=1, keepdims=True)
                halves.append((_dot(e.astype(BF16), v2) / den, m + jnp.log(den)))
            o_ref[rows, cols] = jnp.where(lane_lo, halves[0][0], halves[1][0])
            l_ref[rows, cols] = jnp.where(lane_lo, halves[0][1], halves[1][1])


def _attn_prompt(qv, kv, vv, dil):
    length = qv.shape[0]
    qb = min(length, 512)
    nsub = qb // BLK
    own = pl.BlockSpec((qb, A_WIDTH), lambda r, b: (b, r))
    prev = pl.BlockSpec((BLK, A_WIDTH), lambda r, b: (jnp.maximum(b * nsub - 1, 0), r))
    return pl.pallas_call(
        functools.partial(_attn_kernel, nsub=nsub),
        grid=(dil, length // qb),
        in_specs=[own, own, prev, own, prev],
        out_specs=[own, own],
        out_shape=[jax.ShapeDtypeStruct((length, dil * A_WIDTH), F32)] * 2,
        compiler_params=_params("parallel", "arbitrary"),
        name=f"attn_d{dil}",
    )(qv, kv, kv, vv, vv)


def _attn_sample_kernel(q_ref, ks_ref, vs_ref, k_ref, v_ref, w_ref, o_ref):
    w = w_ref[...]
    live = w > 0.0
    q8 = q_ref[0] * HEAD_DIM**-0.5
    q_cols, v_cols = q8.T, vs_ref[0].T
    s_new = jnp.sum(ks_ref[0] * q8, axis=1, keepdims=True)
    outs = []
    for h in range(A_HEADS):
        q = q_cols[:, h : h + 1]
        s = jnp.where(live, jnp.sum(k_ref[0, h] * q, axis=0, keepdims=True), NEG_INF)
        s_self = s_new[h : h + 1, :]
        m = jnp.maximum(jnp.max(s, axis=1, keepdims=True), s_self)
        e = w * jnp.exp(s - m)
        e_self = len(PATTERNS) * jnp.exp(s_self - m)
        den = jnp.sum(e, axis=1, keepdims=True) + e_self
        acc = jnp.sum(v_ref[0, h] * e, axis=1, keepdims=True) + e_self * v_cols[:, h : h + 1]
        outs.append(acc / den)
    o_ref[0] = jnp.concatenate(outs, axis=1).T


def _attn_sample(q, k, v, ck, cv):
    b, buf = ck.shape[0], ck.shape[1]
    mult = [0.0] * buf
    for window, dil in PATTERNS:
        for j in range(1, window // dil + 1):
            if buf - dil * j >= 0:
                mult[buf - dil * j] += 1.0
    w = jnp.asarray(mult, F32).reshape(1, buf)
    tok = pl.BlockSpec((1, A_HEADS, HEAD_DIM), lambda i: (i, 0, 0))
    cache = pl.BlockSpec((1, A_HEADS, HEAD_DIM, buf), lambda i: (i, 0, 0, 0))
    heads = lambda a: a.reshape(b, A_HEADS, HEAD_DIM)
    pos_minor = lambda a: jnp.transpose(a, (0, 2, 3, 1))
    out = pl.pallas_call(
        _attn_sample_kernel,
        grid=(b,),
        in_specs=[tok, tok, tok, cache, cache, pl.BlockSpec((1, buf), lambda i: (0, 0))],
        out_specs=tok,
        out_shape=jax.ShapeDtypeStruct((b, A_HEADS, HEAD_DIM), F32),
        compiler_params=_params("parallel"),
        name="attn_sample",
    )(heads(q), heads(k), heads(v), pos_minor(ck), pos_minor(cv), w)
    return out.reshape(b, A_WIDTH)


def _merge_kernel(*refs, n_pat):
    x_ref = refs[0]
    att_refs = refs[1 : 1 + 2 * n_pat] if n_pat else refs[1:2]
    gm_ref, ga_ref, gb_ref, wout_ref, g2_ref, wq_ref, sk_ref, h_out, xt_out, st_out = refs[1 + len(att_refs) :][:10]
    scratch = refs[11 + len(att_refs) :]
    tm = x_ref.shape[0]
    if n_pat:
        vals = []
        for idx, ref in enumerate(att_refs):
            dil = PATTERNS[idx % n_pat][1]
            if dil == 1:
                vals.append(ref[...])
                continue
            nat = scratch[idx]
            n_col = A_WIDTH // LANES
            for r in range(dil):
                for j in range(n_col):
                    c0 = r * A_WIDTH + j * LANES
                    nat[j, pl.ds(r, tm // dil, stride=dil), :] = ref[:, c0 : c0 + LANES]
            vals.append(jnp.concatenate([nat[j] for j in range(n_col)], axis=1))
        outs, lses = vals[:n_pat], vals[n_pat:]
        m = functools.reduce(jnp.maximum, lses)
        ws = [jnp.exp(l - m) for l in lses]
        att = sum(w * o for w, o in zip(ws, outs)) / sum(ws)
    else:
        att = att_refs[0][...]
    cat = jnp.concatenate([_rms(att, ga_ref[...]), _rms(gm_ref[...], gb_ref[...])], axis=1)
    h = x_ref[...] + _dot(cat.astype(BF16), wout_ref[...])
    h_out[...] = h
    n2 = _rms(h, g2_ref[...])
    xt_out[...] = n2.T.astype(BF16)
    qt = _dot(n2.astype(BF16), wq_ref[...]).T.astype(BF16)
    for i in range(2 * PEER_HEADS):
        rows = slice(i * PEER_NKEYS, (i + 1) * PEER_NKEYS)
        st_out[rows, :] = _dot(sk_ref[i], qt[rows, :])


def _merge(x, atts, gm, ga, gb, w_out, g2, wq, sub_keys):
    t, d = x.shape
    tm = min(t, 256)
    n_pat = len(atts) // 2
    nq = wq.shape[1]
    sk = sub_keys.reshape(2 * PEER_HEADS, PEER_NKEYS, -1).astype(BF16)
    row = lambda a: a.reshape(1, -1)
    full = lambda a: pl.BlockSpec(a.shape, lambda i: (0,) * a.ndim)
    tok = lambda n: pl.BlockSpec((tm, n), lambda i: (i, 0))
    tokt = lambda n: pl.BlockSpec((n, tm), lambda i: (0, i))
    view = lambda a: pl.BlockSpec((tm * A_WIDTH // a.shape[1], a.shape[1]), lambda i: (i, 0))
    weights = [row(ga), row(gb), w_out.astype(BF16), row(g2), wq.astype(BF16), sk]
    return pl.pallas_call(
        functools.partial(_merge_kernel, n_pat=n_pat),
        grid=(t // tm,),
        in_specs=[tok(d)] + [view(a) for a in atts] + [tok(A_WIDTH)] + [full(a) for a in weights],
        out_specs=[tok(d), tokt(d), tokt(nq)],
        out_shape=[
            jax.ShapeDtypeStruct((t, d), F32),
            jax.ShapeDtypeStruct((d, t), BF16),
            jax.ShapeDtypeStruct((nq, t), F32),
        ],
        scratch_shapes=[pltpu.VMEM((A_WIDTH // LANES, tm, LANES), F32)] * (2 * n_pat),
        compiler_params=_params("parallel"),
        name="merge_prompt" if n_pat else "merge_sample",
    )(x, *atts, gm, *weights)


def _merge_exchange(n):
    pairs, t = [], n.bit_length() - 1
    p = 1 << (t - 1)
    while p:
        q, r, d = 1 << (t - 1), 0, p
        while d:
            pairs += [(i, i + d) for i in range(n - d) if (i & p) == r]
            d, q, r = q - p, q // 2, p
        p //= 2
    return pairs


_SORT_TOPK = _merge_exchange(PEER_TOPK)


def _exchange(x, i, j):
    x[i], x[j] = jnp.maximum(x[i], x[j]), jnp.minimum(x[i], x[j])


def _sublane_all(op, x):
    for shift in (4, 2, 1):
        x = op(x, pltpu.roll(x, shift, 0))
    return x


def _sorted_top16(vals):
    x = list(vals)
    for i, j in _SORT_TOPK:
        _exchange(x, i, j)
    for shift in (4, 2, 1):
        y = [pltpu.roll(v, shift, 0) for v in x]
        x = [jnp.maximum(x[r], y[PEER_TOPK - 1 - r]) for r in range(PEER_TOPK)]
        stride = PEER_TOPK // 2
        while stride:
            for i in range(PEER_TOPK):
                if not i & stride:
                    _exchange(x, i, i + stride)
            stride //= 2
    return x


_CAND_ROWS = ((0, 0, 8), (0, 8, 8), (1, 0, 8), (2, 0, 5), (3, 0, 4), (4, 0, 3), (5, 0, 2), (6, 0, 2), (7, 0, 2))


def _select_fast(a_vals, b_vals):
    k = PEER_TOPK
    one = lambda cond: jnp.where(cond, 1.0, 0.0)
    ta, tb = _sorted_top16(a_vals), _sorted_top16(b_vals)
    sub = lax.broadcasted_iota(jnp.int32, ta[0].shape, 0)
    spread = lambda t, off: functools.reduce(lambda acc, s: jnp.where(sub == s, t[off + s], acc), range(1, 8), t[off])
    tb_lo, tb_hi, ta_hi = spread(tb, 0), spread(tb, 8), spread(ta, 8)
    cands = []
    for k1, k2_0, n_k2 in _CAND_ROWS:
        c = ta[k1] + (tb_hi if k2_0 else tb_lo)
        cands.append(c if n_k2 == 8 else jnp.where(sub < n_k2, c, NEG_INF))
    cands.append(ta_hi + tb[0])
    rem = list(cands)
    for _ in range(k):
        m = _sublane_all(jnp.maximum, functools.reduce(jnp.maximum, rem))
        rem = [jnp.where(c == m, NEG_INF, c) for c in rem]
    chosen = [one(r != c) for r, c in zip(rem, cands)]
    n_chosen = _sublane_all(jnp.add, functools.reduce(jnp.add, chosen))
    top = ta[0] + tb[0]
    z = _sublane_all(jnp.add, functools.reduce(jnp.add, [ch * jnp.exp(c - top) for ch, c in zip(chosen, cands)]))
    n_k1 = [_sublane_all(jnp.add, chosen[0] + chosen[1])] + [_sublane_all(jnp.add, chosen[i]) for i in range(2, 9)]
    n_k1 += [_sublane_all(jnp.add, jnp.where(sub == s, chosen[9], 0.0)) for s in range(8)]
    inv_z = 1.0 / z
    n_at, ea, r2, eb = [], [], [], []
    cnt_a = cnt_b = None
    for av, bv in zip(a_vals, b_vals):
        n = jnp.zeros_like(av)
        for r in range(k):
            n = jnp.where(av == ta[r], n_k1[r], n)
        n_at.append(n)
        ea.append(jnp.exp(av - ta[0]) * inv_z)
        r2.append(functools.reduce(jnp.add, [one(tb[r] > bv) for r in range(k)]))
        eb.append(jnp.exp(bv - tb[0]))
        ca, cb = one(av >= ta[k - 1]), one(bv >= tb[k - 1])
        cnt_a, cnt_b = (ca, cb) if cnt_a is None else (cnt_a + ca, cnt_b + cb)
    flag = one(n_chosen != k) + one(_sublane_all(jnp.add, cnt_a) != k) + one(_sublane_all(jnp.add, cnt_b) != k)
    for r in range(k - 1):
        flag = flag + one(ta[r] == ta[r + 1]) + one(tb[r] == tb[r + 1])
    return n_at, ea, r2, eb, flag


def _top16(a, iota_k):
    rem = a
    rank = jnp.full(a.shape, PEER_TOPK, jnp.int32)
    vals = []
    for r in range(PEER_TOPK):
        m = jnp.max(rem, axis=0, keepdims=True)
        first = jnp.min(jnp.where(rem == m, iota_k, PEER_NKEYS), axis=0, keepdims=True)
        sel = iota_k == first
        rank = jnp.where(sel, r, rank)
        rem = jnp.where(sel, NEG_INF, rem)
        vals.append(m)
    return jnp.concatenate(vals, axis=0), rank


def _select_exact(a, b):
    iota_k = lax.broadcasted_iota(jnp.int32, a.shape, 0)
    ta, r1 = _top16(a, iota_k)
    tb, r2 = _top16(b, iota_k)
    cand = jnp.concatenate([ta[k : k + 1, :] + tb for k in range(PEER_TOPK)], axis=0)
    pos = lax.broadcasted_iota(jnp.int32, cand.shape, 0)
    rem = cand
    chosen = jnp.zeros(cand.shape, F32)
    for _ in range(PEER_TOPK):
        m = jnp.max(rem, axis=0, keepdims=True)
        first = jnp.min(jnp.where(rem == m, pos, cand.shape[0]), axis=0, keepdims=True)
        sel = pos == first
        chosen = jnp.where(sel, 1.0, chosen)
        rem = jnp.where(sel, NEG_INF, rem)
    z = jnp.sum(chosen * jnp.exp(cand - cand[0:1, :]), axis=0, keepdims=True)
    n_at = jnp.zeros(a.shape, F32)
    for k in range(PEER_TOPK):
        n_k = jnp.sum(chosen[k * PEER_TOPK : (k + 1) * PEER_TOPK, :], axis=0, keepdims=True)
        n_at = jnp.where(r1 == k, n_k, n_at)
    return n_at, jnp.exp(a - ta[0:1, :]) / z, r2.astype(F32), jnp.exp(b - tb[0:1, :])


def _select_kernel(st_ref, na_ref, ea_ref, r2_ref, eb_ref):
    tk = st_ref.shape[1]
    n_tile = PEER_NKEYS // 8

    def head(h, carry):
        base = pl.multiple_of(h * (2 * PEER_NKEYS), 2 * PEER_NKEYS)
        for lt in range(tk // LANES):
            lanes = slice(lt * LANES, (lt + 1) * LANES)
            a_vals = [st_ref[pl.ds(base + 8 * v, 8), lanes] for v in range(n_tile)]
            b_vals = [st_ref[pl.ds(base + PEER_NKEYS + 8 * v, 8), lanes] for v in range(n_tile)]
            n_at, ea, r2, eb, flag = _select_fast(a_vals, b_vals)
            stack = lambda tiles: jnp.concatenate(tiles, axis=0)
            na_ref[h, :, lanes] = stack(n_at)
            ea_ref[h, :, lanes] = stack(ea)
            r2_ref[h, :, lanes] = stack(r2).astype(BF16)
            eb_ref[h, :, lanes] = stack(eb).astype(BF16)

            @pl.when(jnp.max(flag) > 0.0)
            def _():
                n_x, ea_x, r2_x, eb_x = _select_exact(st_ref[pl.ds(base, PEER_NKEYS), lanes], st_ref[pl.ds(base + PEER_NKEYS, PEER_NKEYS), lanes])
                na_ref[h, :, lanes] = n_x
                ea_ref[h, :, lanes] = ea_x
                r2_ref[h, :, lanes] = r2_x.astype(BF16)
                eb_ref[h, :, lanes] = eb_x.astype(BF16)

        return carry

    lax.fori_loop(0, PEER_HEADS, head, 0)


def _select(st):
    t = st.shape[1]
    tk = min(t, 256)
    out = pl.BlockSpec((PEER_HEADS, PEER_NKEYS, tk), lambda i: (0, 0, i))
    return pl.pallas_call(
        _select_kernel,
        grid=(t // tk,),
        in_specs=[pl.BlockSpec((2 * PEER_HEADS * PEER_NKEYS, tk), lambda i: (0, i))],
        out_specs=[out] * 4,
        out_shape=[jax.ShapeDtypeStruct((PEER_HEADS, PEER_NKEYS, t), dt) for dt in (F32, F32, BF16, BF16)],
        compiler_params=_params("parallel"),
        name="peer_select",
    )(st)


def _peer_kernel(u_ref, xt_ref, vt_ref, na_ref, ea_ref, r2_ref, eb_ref, o_ref, acc_ref, gt_ref):
    j = pl.program_id(1)

    @pl.when(j == 0)
    def _():
        acc_ref[...] = jnp.zeros_like(acc_ref)

    tm = xt_ref.shape[1]
    sub = BF16_SUBLANES
    row_tile = lambda ref, h, ii: jnp.broadcast_to(ref[h, ii : ii + 1, :], (sub, tm)).astype(BF16)[None]
    tiles = lambda a: a.reshape(PEER_NKEYS // sub, sub, tm)
    st = _dot(u_ref[...], xt_ref[...])
    for ii in range(u_ref.shape[0] // PEER_NKEYS):
        w = None
        for h in range(PEER_HEADS):
            hit = tiles(r2_ref[h]) < row_tile(na_ref, h, ii)
            t = jnp.where(hit, tiles(eb_ref[h]), jnp.zeros((), BF16)) * row_tile(ea_ref, h, ii)
            w = t if w is None else w + t
        rows = slice(ii * PEER_NKEYS, (ii + 1) * PEER_NKEYS)
        g = tiles(jax.nn.gelu(st[rows, :]).astype(BF16)) * w
        gt_ref[rows, :] = g.reshape(PEER_NKEYS, tm)
    acc_ref[...] += _dot(vt_ref[...], gt_ref[...])

    @pl.when(j == pl.num_programs(1) - 1)
    def _():
        o_ref[...] = acc_ref[...].T


def _peer(xt, u, vt, na, ea, r2, eb):
    d, t = xt.shape
    e = u.shape[0]
    tm = min(t, 512)
    te = 1024
    i1_per = te // PEER_NKEYS
    sel_i1 = pl.BlockSpec((PEER_HEADS, i1_per, tm), lambda i, j: (0, j, i))
    sel_i2 = pl.BlockSpec((PEER_HEADS, PEER_NKEYS, tm), lambda i, j: (0, 0, i))
    return pl.pallas_call(
        _peer_kernel,
        grid=(t // tm, e // te),
        in_specs=[
            pl.BlockSpec((te, d), lambda i, j: (j, 0)),
            pl.BlockSpec((d, tm), lambda i, j: (0, i)),
            pl.BlockSpec((d, te), lambda i, j: (0, j)),
            sel_i1,
            sel_i1,
            sel_i2,
            sel_i2,
        ],
        out_specs=pl.BlockSpec((tm, d), lambda i, j: (i, 0)),
        out_shape=jax.ShapeDtypeStruct((t, d), F32),
        scratch_shapes=[pltpu.VMEM((d, tm), F32), pltpu.VMEM((te, tm), BF16)],
        compiler_params=_params("parallel", "arbitrary"),
        name="peer_dense",
    )(u, xt, vt, na, ea, r2, eb)


def _final_kernel(h_ref, po_ref, p_ref, g3_ref, wg_ref, wp_ref, y_ref):
    h = h_ref[...] + po_ref[...]
    n3 = _rms(h, g3_ref[...])
    gate = jax.nn.sigmoid(_dot(n3.astype(BF16), wg_ref[...]))
    y_ref[...] = h + gate * _dot(p_ref[...].astype(BF16), wp_ref[...])


def _final(h, po, p, g3, wg, wp):
    t, d = h.shape
    tm = min(t, 256)
    tok = lambda n: pl.BlockSpec((tm, n), lambda i: (i, 0))
    full = lambda a: pl.BlockSpec(a.shape, lambda i: (0,) * a.ndim)
    weights = [g3.reshape(1, -1), wg.astype(BF16), wp.astype(BF16)]
    return pl.pallas_call(
        _final_kernel,
        grid=(t // tm,),
        in_specs=[tok(d), tok(d), tok(p.shape[1])] + [full(a) for a in weights],
        out_specs=tok(d),
        out_shape=jax.ShapeDtypeStruct((t, d), F32),
        compiler_params=_params("parallel"),
        name="final",
    )(h, po, p, *weights)


def kernel(x_prompt, x_sample, cache_k, cache_v, p_prompt, p_sample, norm1_g, w_in, q_norm_g, k_norm_g, v_norm_g, spatial_w, spatial_b, out_norm_a_g, out_norm_b_g, w_out, norm2_g, peer_w_query, peer_sub_keys, peer_u, peer_v, norm3_g, ple_w_gate, ple_w_proj):
    depth = norm1_g.shape[0]
    bsz, s, d = x_prompt.shape
    db, dt, _ = x_sample.shape
    buf = cache_k.shape[2]
    assert bsz == 1 and dt == 1, "one prompt sequence and single-token decode rows"
    assert buf == MAX_WINDOW and s % (PATTERNS[-1][1] * BLK) == 0 and s >= MAX_WINDOW
    assert all(window == dil * BLK for window, dil in PATTERNS)
    assert PATTERNS[0][1] == 1
    hp, hs = x_prompt[0], x_sample[:, 0]
    outs = [[] for _ in range(5)]
    for l in range(depth):
        u_bf = peer_u[l].astype(BF16)
        vt_bf = peer_v[l].astype(BF16).T
        vg = v_norm_g[l].reshape(-1)

        def tail(h, atts, gm, p):
            h1, xt, st = _merge(h, atts, gm, out_norm_a_g[l], out_norm_b_g[l], w_out[l], norm2_g[l], peer_w_query[l], peer_sub_keys[l])
            po = _peer(xt, u_bf, vt_bf, *_select(st))
            return _final(h1, po, p, norm3_g[l], ple_w_gate[l], ple_w_proj[l])

        q, k, v, gm, *views = _proj(hp, 0, norm1_g[l], w_in[l], q_norm_g[l], k_norm_g[l], vg, spatial_w[l], spatial_b[l], False)
        per_pat = [_attn_prompt(q, k, v, 1)] + [_attn_prompt(*views[3 * i : 3 * i + 3], dil) for i, (_, dil) in enumerate(PATTERNS[1:])]
        hp = tail(hp, [o for o, _ in per_pat] + [lse for _, lse in per_pat], gm, p_prompt[l, 0])
        keep = min(MAX_WINDOW, s)
        outs[0].append(k[s - keep :].reshape(1, keep, A_HEADS, HEAD_DIM))
        outs[1].append(v[s - keep :].reshape(1, keep, A_HEADS, HEAD_DIM))

        q, k, v, gm, vn = _proj(hs, PAST_LEN, norm1_g[l], w_in[l], q_norm_g[l], k_norm_g[l], vg, spatial_w[l], spatial_b[l], True)
        att = _attn_sample(q, k, v, cache_k[l], cache_v[l])
        hs = tail(hs, [att], gm, p_sample[l, :, 0])
        for o, a in zip(outs[2:], (k, v, vn)):
            o.append(a.reshape(db, 1, A_HEADS, HEAD_DIM))
    return (hp[None], hs[:, None], *(jnp.stack(o) for o in outs))
```

```python
import functools

import jax
import jax.numpy as jnp
from jax import lax
from jax.experimental import pallas as pl
from jax.experimental.pallas import tpu as pltpu

HEAD_DIM = 64
A_HEADS = 8
A_WIDTH = A_HEADS * HEAD_DIM
B_GROUPS = 8
B_WIDTH = B_GROUPS * HEAD_DIM
PATTERNS = ((128, 1), (512, 4), (2048, 16))
MAX_WINDOW = 2048
PAST_LEN = 8192
BLK = 128
ROT_DIM = HEAD_DIM // 4
ROPE_THETA = 500000.0
PEER_HEADS = 8
PEER_NKEYS = 128
PEER_TOPK = 16
EPS = 1e-6
LANES = 128
BF16_SUBLANES = 16
ROW_TILE = 512

VMEM_LIMIT = 48 * 1024 * 1024

F32 = jnp.float32
BF16 = jnp.bfloat16
NEG_INF = float("-inf")


def _params(*sem, flags=None):
    return pltpu.CompilerParams(dimension_semantics=sem, vmem_limit_bytes=VMEM_LIMIT, flags=flags)


def _dot(a, b):
    return jnp.dot(a, b, preferred_element_type=F32)


def _dot_split(a, m, passes):
    acc = None
    rem = a
    for _ in range(passes):
        piece = rem.astype(BF16)
        rem = rem - piece.astype(F32)
        t = _dot(piece, m)
        acc = t if acc is None else acc + t
    return acc


def _rms(x, g):
    return x * lax.rsqrt(jnp.mean(x * x, axis=-1, keepdims=True) + EPS) * g


def _gelu_tanh(x):
    c0 = 2.0 * (2.0 / jnp.pi) ** 0.5
    return x / (1.0 + jnp.exp(x * (-c0 - (c0 * 0.044715) * (x * x))))


def _tile4(t):
    return jnp.concatenate([t, t, t, t], axis=1)


def _proj_kernel(x_ref, g1_ref, win_ref, qg_ref, kg_ref, vg_ref, base_ref, cr_ref, sr_ref, ma_ref, mb_ref, mh_ref, *rest, sample):
    if sample:
        w00_ref, b0_ref, q_out, k_out, v_out, gm_out, vn_out = rest
    else:
        ws_ref, bias_ref, q_out, k_out, v_out, gm_out, *view_outs, slab_ref = rest
    tm = x_ref.shape[0]
    n1 = _rms(x_ref[...], g1_ref[...])
    z = _dot(n1.astype(BF16), win_ref[...])
    mh = mh_ref[...]
    cos_b, sin_b0 = base_ref[0, 0:1, :], base_ref[0, 1:2, :]
    cos_r, sin_r = cr_ref[...], sr_ref[...]
    cos_p = cos_b * cos_r - sin_b0 * sin_r
    sin_p = sin_b0 * cos_r + cos_b * sin_r
    cos_t, sin_a, sin_b = _tile4(cos_p), _tile4(sin_p * ma_ref[...]), _tile4(sin_p * mb_ref[...])

    def head_norm_rope(t, g):
        tn = t * lax.rsqrt(_dot_split(t * t, mh, 2) + EPS) * g
        return tn * cos_t + pltpu.roll(tn, A_WIDTH - ROT_DIM // 2, 1) * sin_a + pltpu.roll(tn, ROT_DIM // 2, 1) * sin_b

    q_out[...] = head_norm_rope(z[:, 0:A_WIDTH], qg_ref[...])
    k_out[...] = head_norm_rope(z[:, A_WIDTH : 2 * A_WIDTH], kg_ref[...])
    v_out[...] = z[:, 2 * A_WIDTH : 3 * A_WIDTH]
    u = jax.nn.gelu(z[:, 3 * A_WIDTH : 3 * A_WIDTH + B_WIDTH])
    gv = jax.nn.gelu(z[:, 3 * A_WIDTH + B_WIDTH :])
    vn = gv * lax.rsqrt(_dot_split(gv * gv, mh, 2) + EPS) * vg_ref[...]
    if sample:
        vn_out[...] = vn
        gm_out[...] = u * (vn * w00_ref[...] + b0_ref[...])
    else:
        row = lax.broadcasted_iota(jnp.int32, (BLK, BLK), 0)
        col = lax.broadcasted_iota(jnp.int32, (BLK, BLK), 1)
        lane_lo = col < HEAD_DIM
        causal = row >= col
        wm = [jnp.where(causal, ws_ref[g], 0.0).astype(BF16) for g in range(B_GROUPS)]
        vnb = vn.astype(BF16)
        for c in range(tm // BLK):
            pieces = []
            for j in range(B_WIDTH // LANES):
                s = vnb[c * BLK : (c + 1) * BLK, j * LANES : (j + 1) * LANES]
                pieces.append(jnp.where(lane_lo, _dot(wm[2 * j], s), _dot(wm[2 * j + 1], s)))
            mixed = jnp.concatenate(pieces, axis=1) + bias_ref[...]
            gm_out[c * BLK : (c + 1) * BLK, :] = u[c * BLK : (c + 1) * BLK, :] * mixed
        n_col = A_WIDTH // LANES
        for ai, src in enumerate((q_out, k_out, v_out)):
            for j in range(n_col):
                slab_ref[j] = src[:, j * LANES : (j + 1) * LANES]
            for vi, (_, dil) in enumerate(PATTERNS[1:]):
                dst = view_outs[3 * vi + ai]
                for r in range(dil):
                    for j in range(n_col):
                        c0 = r * A_WIDTH + j * LANES
                        dst[:, c0 : c0 + LANES] = slab_ref[j, pl.ds(r, tm // dil, stride=dil), :].astype(BF16)


def _proj(x, pos0, g1, w_in, qg, kg, vg, ws, bs, sample):
    t, d = x.shape
    tm = min(t, ROW_TILE)
    nt = t // tm
    half = ROT_DIM // 2
    inv = ROPE_THETA ** (-jnp.arange(0, ROT_DIM, 2, dtype=F32) / ROT_DIM)
    lane = jnp.arange(LANES) % HEAD_DIM
    freq = jnp.where(lane < ROT_DIM, inv[lane % half], 0.0)
    ma = jnp.where(lane < half, -1.0, 0.0).reshape(1, -1)
    mb = jnp.where((lane >= half) & (lane < ROT_DIM), 1.0, 0.0).reshape(1, -1)
    step = 0 if sample else tm
    ang_base = (pos0 + step * jnp.arange(nt)).astype(F32)[:, None] * freq[None, :]
    base = jnp.zeros((nt, 8, LANES), F32).at[:, 0].set(jnp.cos(ang_base)).at[:, 1].set(jnp.sin(ang_base))
    ang_row = (jnp.zeros((tm,), F32) if sample else jnp.arange(tm, dtype=F32))[:, None] * freq[None, :]
    head_id = jnp.arange(A_WIDTH) // HEAD_DIM
    mh = ((head_id[:, None] == head_id[None, :]).astype(F32) / HEAD_DIM).astype(BF16)
    row = lambda a: a.reshape(1, -1)
    tiled = lambda a: jnp.tile(a, A_HEADS).reshape(1, -1)
    full = lambda a: pl.BlockSpec(a.shape, lambda i: (0,) * a.ndim)
    tok = lambda n: pl.BlockSpec((tm, n), lambda i: (i, 0))
    args = [x, row(g1), w_in.astype(BF16), tiled(qg), tiled(kg), row(vg), base, jnp.cos(ang_row), jnp.sin(ang_row), ma, mb, mh]
    specs = [tok(d)] + [full(a) for a in args[1:6]] + [pl.BlockSpec((1, 8, LANES), lambda i: (i, 0, 0))] + [full(a) for a in args[7:]]
    if sample:
        extra = [jnp.repeat(ws[:, 0, 0], HEAD_DIM).reshape(1, -1), jnp.repeat(bs[:, 0], HEAD_DIM).reshape(1, -1)]
        out_specs = [tok(A_WIDTH)] * 5
        out_shape = [jax.ShapeDtypeStruct((t, A_WIDTH), F32)] * 5
    else:
        extra = [ws, jnp.repeat(bs.T, HEAD_DIM, axis=1)]
        out_specs = [tok(A_WIDTH)] * 4
        out_shape = [jax.ShapeDtypeStruct((t, A_WIDTH), F32)] * 4
        for _, dil in PATTERNS[1:]:
            out_specs += [pl.BlockSpec((tm // dil, dil * A_WIDTH), lambda i: (i, 0))] * 3
            out_shape += [jax.ShapeDtypeStruct((t // dil, dil * A_WIDTH), BF16)] * 3
    args += extra
    specs += [full(a) for a in extra]
    return pl.pallas_call(
        functools.partial(_proj_kernel, sample=sample),
        grid=(nt,),
        in_specs=specs,
        out_specs=out_specs,
        out_shape=out_shape,
        scratch_shapes=[] if sample else [pltpu.VMEM((A_WIDTH // LANES, tm, LANES), F32)],
        compiler_params=_params("parallel"),
        name="proj_sample" if sample else "proj_prompt",
    )(*args)


def _attn_kernel(q_ref, k_ref, kp_ref, v_ref, vp_ref, o_ref, l_ref, *, nsub):
    first = pl.program_id(1) == 0
    qi = lax.broadcasted_iota(jnp.int32, (BLK, 2 * BLK), 0)
    kj = lax.broadcasted_iota(jnp.int32, (BLK, 2 * BLK), 1)
    dist = BLK + qi - kj
    band = (dist >= 0) & (dist <= BLK)
    band_first = band & ((kj >= BLK) | jnp.logical_not(first))
    lane_lo = lax.broadcasted_iota(jnp.int32, (BLK, LANES), 1) < HEAD_DIM
    scale = HEAD_DIM**-0.5
    for sb in range(nsub):
        r0 = sb * BLK
        rows = slice(r0, r0 + BLK)
        prev = slice(r0 - BLK, r0)
        valid = band_first if sb == 0 else band
        for j in range(A_WIDTH // LANES):
            cols = slice(j * LANES, (j + 1) * LANES)
            qf = q_ref[rows, cols]
            kprev = kp_ref[:, cols] if sb == 0 else k_ref[prev, cols]
            vprev = vp_ref[:, cols] if sb == 0 else v_ref[prev, cols]
            k2 = jnp.concatenate([kprev, k_ref[rows, cols]], axis=0).astype(BF16)
            v2 = jnp.concatenate([vprev, v_ref[rows, cols]], axis=0).astype(BF16)
            halves = []
            for lo in (True, False):
                qm = jnp.where(lane_lo == lo, qf, jnp.zeros((), qf.dtype)).astype(BF16)
                s = lax.dot_general(qm, k2, (((1,), (1,)), ((), ())), preferred_element_type=F32) * scale
                s = jnp.where(valid, s, NEG_INF)
                m = jnp.max(s, axis=1, keepdims=True)
                e = jnp.exp(s - m)
                den = jnp.sum(e, axis=1, keepdims=True)
                halves.append((_dot(e.astype(BF16), v2) / den, m + jnp.log(den)))
            o_ref[rows, cols] = jnp.where(lane_lo, halves[0][0], halves[1][0])
            l_ref[rows, cols] = jnp.where(lane_lo, halves[0][1], halves[1][1])


def _attn_prompt(qv, kv, vv, dil):
    length = qv.shape[0]
    qb = min(length, 512)
    nsub = qb // BLK
    own = pl.BlockSpec((qb, A_WIDTH), lambda r, b: (b, r))
    prev = pl.BlockSpec((BLK, A_WIDTH), lambda r, b: (jnp.maximum(b * nsub - 1, 0), r))
    return pl.pallas_call(
        functools.partial(_attn_kernel, nsub=nsub),
        grid=(dil, length // qb),
        in_specs=[own, own, prev, own, prev],
        out_specs=[own, own],
        out_shape=[jax.ShapeDtypeStruct((length, dil * A_WIDTH), F32)] * 2,
        compiler_params=_params("parallel", "arbitrary"),
        name=f"attn_d{dil}",
    )(qv, kv, kv, vv, vv)


def _attn_sample_kernel(q_ref, ks_ref, vs_ref, k_ref, v_ref, w_ref, o_ref):
    w = w_ref[...]
    live = w > 0.0
    q8 = q_ref[0] * HEAD_DIM**-0.5
    q_cols, v_cols = q8.T, vs_ref[0].T
    s_new = jnp.sum(ks_ref[0] * q8, axis=1, keepdims=True)
    outs = []
    for h in range(A_HEADS):
        q = q_cols[:, h : h + 1]
        s = jnp.where(live, jnp.sum(k_ref[0, h] * q, axis=0, keepdims=True), NEG_INF)
        s_self = s_new[h : h + 1, :]
        m = jnp.maximum(jnp.max(s, axis=1, keepdims=True), s_self)
        e = w * jnp.exp(s - m)
        e_self = len(PATTERNS) * jnp.exp(s_self - m)
        den = jnp.sum(e, axis=1, keepdims=True) + e_self
        acc = jnp.sum(v_ref[0, h] * e, axis=1, keepdims=True) + e_self * v_cols[:, h : h + 1]
        outs.append(acc / den)
    o_ref[0] = jnp.concatenate(outs, axis=1).T


def _attn_sample(q, k, v, ck, cv):
    b, buf = ck.shape[0], ck.shape[1]
    mult = [0.0] * buf
    for window, dil in PATTERNS:
        for j in range(1, window // dil + 1):
            if buf - dil * j >= 0:
                mult[buf - dil * j] += 1.0
    w = jnp.asarray(mult, F32).reshape(1, buf)
    tok = pl.BlockSpec((1, A_HEADS, HEAD_DIM), lambda i: (i, 0, 0))
    cache = pl.BlockSpec((1, A_HEADS, HEAD_DIM, buf), lambda i: (i, 0, 0, 0))
    heads = lambda a: a.reshape(b, A_HEADS, HEAD_DIM)
    pos_minor = lambda a: jnp.transpose(a, (0, 2, 3, 1))
    out = pl.pallas_call(
        _attn_sample_kernel,
        grid=(b,),
        in_specs=[tok, tok, tok, cache, cache, pl.BlockSpec((1, buf), lambda i: (0, 0))],
        out_specs=tok,
        out_shape=jax.ShapeDtypeStruct((b, A_HEADS, HEAD_DIM), F32),
        compiler_params=_params("parallel"),
        name="attn_sample",
    )(heads(q), heads(k), heads(v), pos_minor(ck), pos_minor(cv), w)
    return out.reshape(b, A_WIDTH)


def _merge_kernel(*refs, n_pat):
    x_ref = refs[0]
    att_refs = refs[1 : 1 + 2 * n_pat] if n_pat else refs[1:2]
    gm_ref, ga_ref, gb_ref, wout_ref, g2_ref, wq_ref, sk_ref, h_out, xt_out, st_out = refs[1 + len(att_refs) :][:10]
    scratch = refs[11 + len(att_refs) :]
    tm = x_ref.shape[0]
    if n_pat:
        vals = []
        for idx, ref in enumerate(att_refs):
            dil = PATTERNS[idx % n_pat][1]
            if dil == 1:
                vals.append(ref[...])
                continue
            nat = scratch[idx]
            n_col = A_WIDTH // LANES
            for r in range(dil):
                for j in range(n_col):
                    c0 = r * A_WIDTH + j * LANES
                    nat[j, pl.ds(r, tm // dil, stride=dil), :] = ref[:, c0 : c0 + LANES]
            vals.append(jnp.concatenate([nat[j] for j in range(n_col)], axis=1))
        outs, lses = vals[:n_pat], vals[n_pat:]
        m = functools.reduce(jnp.maximum, lses)
        ws = [jnp.exp(l - m) for l in lses]
        att = sum(w * o for w, o in zip(ws, outs)) / sum(ws)
    else:
        att = att_refs[0][...]
    cat = jnp.concatenate([_rms(att, ga_ref[...]), _rms(gm_ref[...], gb_ref[...])], axis=1)
    h = x_ref[...] + _dot(cat.astype(BF16), wout_ref[...])
    h_out[...] = h
    n2 = _rms(h, g2_ref[...])
    xt_out[...] = n2.T.astype(BF16)
    qt = _dot(n2.astype(BF16), wq_ref[...]).T.astype(BF16)
    for i in range(2 * PEER_HEADS):
        rows = slice(i * PEER_NKEYS, (i + 1) * PEER_NKEYS)
        st_out[rows, :] = _dot(sk_ref[i], qt[rows, :])


def _merge(x, atts, gm, ga, gb, w_out, g2, wq, sub_keys):
    t, d = x.shape
    tm = min(t, ROW_TILE)
    n_pat = len(atts) // 2
    nq = wq.shape[1]
    sk = sub_keys.reshape(2 * PEER_HEADS, PEER_NKEYS, -1).astype(BF16)
    row = lambda a: a.reshape(1, -1)
    full = lambda a: pl.BlockSpec(a.shape, lambda i: (0,) * a.ndim)
    tok = lambda n: pl.BlockSpec((tm, n), lambda i: (i, 0))
    tokt = lambda n: pl.BlockSpec((n, tm), lambda i: (0, i))
    view = lambda a: pl.BlockSpec((tm * A_WIDTH // a.shape[1], a.shape[1]), lambda i: (i, 0))
    weights = [row(ga), row(gb), w_out.astype(BF16), row(g2), wq.astype(BF16), sk]
    return pl.pallas_call(
        functools.partial(_merge_kernel, n_pat=n_pat),
        grid=(t // tm,),
        in_specs=[tok(d)] + [view(a) for a in atts] + [tok(A_WIDTH)] + [full(a) for a in weights],
        out_specs=[tok(d), tokt(d), tokt(nq)],
        out_shape=[
            jax.ShapeDtypeStruct((t, d), F32),
            jax.ShapeDtypeStruct((d, t), BF16),
            jax.ShapeDtypeStruct((nq, t), F32),
        ],
        scratch_shapes=[pltpu.VMEM((A_WIDTH // LANES, tm, LANES), F32)] * (2 * n_pat),
        compiler_params=_params("parallel"),
        name="merge_prompt" if n_pat else "merge_sample",
    )(x, *atts, gm, *weights)


def _merge_exchange(n):
    pairs, t = [], n.bit_length() - 1
    p = 1 << (t - 1)
    while p:
        q, r, d = 1 << (t - 1), 0, p
        while d:
            pairs += [(i, i + d) for i in range(n - d) if (i & p) == r]
            d, q, r = q - p, q // 2, p
        p //= 2
    return pairs


_SORT_TOPK = _merge_exchange(PEER_TOPK)


def _exchange(x, i, j):
    x[i], x[j] = jnp.maximum(x[i], x[j]), jnp.minimum(x[i], x[j])


def _sublane_all(op, x):
    for shift in (4, 2, 1):
        x = op(x, pltpu.roll(x, shift, 0))
    return x


def _sorted_top16(vals):
    x = list(vals)
    for i, j in _SORT_TOPK:
        _exchange(x, i, j)
    for shift in (4, 2, 1):
        y = [pltpu.roll(v, shift, 0) for v in x]
        x = [jnp.maximum(x[r], y[PEER_TOPK - 1 - r]) for r in range(PEER_TOPK)]
        stride = PEER_TOPK // 2
        while stride:
            for i in range(PEER_TOPK):
                if not i & stride:
                    _exchange(x, i, i + stride)
            stride //= 2
    return x


_CAND_ROWS = ((0, 0, 8), (0, 8, 8), (1, 0, 8), (2, 0, 5), (3, 0, 4), (4, 0, 3), (5, 0, 2), (6, 0, 2), (7, 0, 2))


def _select_fast(a_vals, b_vals):
    k = PEER_TOPK
    one = lambda cond: jnp.where(cond, 1.0, 0.0)
    ta, tb = _sorted_top16(a_vals), _sorted_top16(b_vals)
    sub = lax.broadcasted_iota(jnp.int32, ta[0].shape, 0)
    spread = lambda t, off: functools.reduce(lambda acc, s: jnp.where(sub == s, t[off + s], acc), range(1, 8), t[off])
    tb_lo, tb_hi, ta_hi = spread(tb, 0), spread(tb, 8), spread(ta, 8)
    cands = []
    for k1, k2_0, n_k2 in _CAND_ROWS:
        c = ta[k1] + (tb_hi if k2_0 else tb_lo)
        cands.append(c if n_k2 == 8 else jnp.where(sub < n_k2, c, NEG_INF))
    cands.append(ta_hi + tb[0])
    rem = list(cands)
    for _ in range(k):
        m = _sublane_all(jnp.maximum, functools.reduce(jnp.maximum, rem))
        rem = [jnp.where(c == m, NEG_INF, c) for c in rem]
    chosen = [one(r != c) for r, c in zip(rem, cands)]
    n_chosen = _sublane_all(jnp.add, functools.reduce(jnp.add, chosen))
    top = ta[0] + tb[0]
    z = _sublane_all(jnp.add, functools.reduce(jnp.add, [ch * jnp.exp(c - top) for ch, c in zip(chosen, cands)]))
    n_k1 = [_sublane_all(jnp.add, chosen[0] + chosen[1])] + [_sublane_all(jnp.add, chosen[i]) for i in range(2, 9)]
    n_k1 += [_sublane_all(jnp.add, jnp.where(sub == s, chosen[9], 0.0)) for s in range(8)]
    inv_z = 1.0 / z
    n_at, ea, r2, eb = [], [], [], []
    cnt_a = cnt_b = None
    for av, bv in zip(a_vals, b_vals):
        n = jnp.zeros_like(av)
        for r in range(k):
            n = jnp.where(av == ta[r], n_k1[r], n)
        n_at.append(n)
        ea.append(jnp.exp(av - ta[0]) * inv_z)
        rank = jnp.zeros_like(bv)
        for r in range(k):
            rank = jnp.where(tb[r] > bv, r + 1.0, rank)
        r2.append(rank)
        eb.append(jnp.exp(bv - tb[0]))
        ca, cb = one(av >= ta[k - 1]), one(bv >= tb[k - 1])
        cnt_a, cnt_b = (ca, cb) if cnt_a is None else (cnt_a + ca, cnt_b + cb)
    flag = one(n_chosen != k) + one(_sublane_all(jnp.add, cnt_a) != k) + one(_sublane_all(jnp.add, cnt_b) != k)
    for r in range(k - 1):
        flag = flag + one(ta[r] == ta[r + 1]) + one(tb[r] == tb[r + 1])
    return n_at, ea, r2, eb, flag


def _top16(a, iota_k):
    rem = a
    rank = jnp.full(a.shape, PEER_TOPK, jnp.int32)
    vals = []
    for r in range(PEER_TOPK):
        m = jnp.max(rem, axis=0, keepdims=True)
        first = jnp.min(jnp.where(rem == m, iota_k, PEER_NKEYS), axis=0, keepdims=True)
        sel = iota_k == first
        rank = jnp.where(sel, r, rank)
        rem = jnp.where(sel, NEG_INF, rem)
        vals.append(m)
    return jnp.concatenate(vals, axis=0), rank


def _select_exact(a, b):
    iota_k = lax.broadcasted_iota(jnp.int32, a.shape, 0)
    ta, r1 = _top16(a, iota_k)
    tb, r2 = _top16(b, iota_k)
    cand = jnp.concatenate([ta[k : k + 1, :] + tb for k in range(PEER_TOPK)], axis=0)
    pos = lax.broadcasted_iota(jnp.int32, cand.shape, 0)
    rem = cand
    chosen = jnp.zeros(cand.shape, F32)
    for _ in range(PEER_TOPK):
        m = jnp.max(rem, axis=0, keepdims=True)
        first = jnp.min(jnp.where(rem == m, pos, cand.shape[0]), axis=0, keepdims=True)
        sel = pos == first
        chosen = jnp.where(sel, 1.0, chosen)
        rem = jnp.where(sel, NEG_INF, rem)
    z = jnp.sum(chosen * jnp.exp(cand - cand[0:1, :]), axis=0, keepdims=True)
    n_at = jnp.zeros(a.shape, F32)
    for k in range(PEER_TOPK):
        n_k = jnp.sum(chosen[k * PEER_TOPK : (k + 1) * PEER_TOPK, :], axis=0, keepdims=True)
        n_at = jnp.where(r1 == k, n_k, n_at)
    return n_at, jnp.exp(a - ta[0:1, :]) / z, r2.astype(F32), jnp.exp(b - tb[0:1, :])


def _select_kernel(st_ref, na_ref, ea_ref, r2_ref, eb_ref):
    tk = st_ref.shape[1]
    n_tile = PEER_NKEYS // 8

    def head(h, carry):
        base = pl.multiple_of(h * (2 * PEER_NKEYS), 2 * PEER_NKEYS)
        for lt in range(tk // LANES):
            lanes = slice(lt * LANES, (lt + 1) * LANES)
            a_vals = [st_ref[pl.ds(base + 8 * v, 8), lanes] for v in range(n_tile)]
            b_vals = [st_ref[pl.ds(base + PEER_NKEYS + 8 * v, 8), lanes] for v in range(n_tile)]
            n_at, ea, r2, eb, flag = _select_fast(a_vals, b_vals)
            stack = lambda tiles: jnp.concatenate(tiles, axis=0)
            na_ref[h, :, lanes] = stack(n_at)
            ea_ref[h, :, lanes] = stack(ea)
            r2_ref[h, :, lanes] = stack(r2).astype(BF16)
            eb_ref[h, :, lanes] = stack(eb).astype(BF16)

            @pl.when(jnp.max(flag) > 0.0)
            def _():
                n_x, ea_x, r2_x, eb_x = _select_exact(st_ref[pl.ds(base, PEER_NKEYS), lanes], st_ref[pl.ds(base + PEER_NKEYS, PEER_NKEYS), lanes])
                na_ref[h, :, lanes] = n_x
                ea_ref[h, :, lanes] = ea_x
                r2_ref[h, :, lanes] = r2_x.astype(BF16)
                eb_ref[h, :, lanes] = eb_x.astype(BF16)

        return carry

    lax.fori_loop(0, PEER_HEADS, head, 0)


def _select(st):
    t = st.shape[1]
    tk = min(t, 256)
    out = pl.BlockSpec((PEER_HEADS, PEER_NKEYS, tk), lambda i: (0, 0, i))
    return pl.pallas_call(
        _select_kernel,
        grid=(t // tk,),
        in_specs=[pl.BlockSpec((2 * PEER_HEADS * PEER_NKEYS, tk), lambda i: (0, i))],
        out_specs=[out] * 4,
        out_shape=[jax.ShapeDtypeStruct((PEER_HEADS, PEER_NKEYS, t), dt) for dt in (F32, F32, BF16, BF16)],
        compiler_params=_params("parallel"),
        name="peer_select",
    )(st)


def _peer_kernel(u_ref, xt_ref, vt_ref, na_ref, ea_ref, r2_ref, eb_ref, o_ref, acc_ref, gt_ref):
    j = pl.program_id(1)

    @pl.when(j == 0)
    def _():
        acc_ref[...] = jnp.zeros_like(acc_ref)

    tm = xt_ref.shape[1]
    sub = BF16_SUBLANES
    row_tile = lambda ref, h, ii: jnp.broadcast_to(ref[h, ii : ii + 1, :], (sub, tm)).astype(BF16)[None]
    tiles = lambda a: a.reshape(PEER_NKEYS // sub, sub, tm)
    st = _dot(u_ref[...], xt_ref[...])
    for ii in range(u_ref.shape[0] // PEER_NKEYS):
        w = None
        for h in range(PEER_HEADS):
            hit = tiles(r2_ref[h]) < row_tile(na_ref, h, ii)
            t = jnp.where(hit, tiles(eb_ref[h]), jnp.zeros((), BF16)) * row_tile(ea_ref, h, ii)
            w = t if w is None else w + t
        rows = slice(ii * PEER_NKEYS, (ii + 1) * PEER_NKEYS)
        g = tiles(_gelu_tanh(st[rows, :]).astype(BF16)) * w
        gt_ref[rows, :] = g.reshape(PEER_NKEYS, tm)
    acc_ref[...] += _dot(vt_ref[...], gt_ref[...])

    @pl.when(j == pl.num_programs(1) - 1)
    def _():
        o_ref[...] = acc_ref[...].T


def _peer(xt, u, vt, na, ea, r2, eb):
    d, t = xt.shape
    e = u.shape[0]
    tm = min(t, 1024)
    te = 1024
    i1_per = te // PEER_NKEYS
    sel_i1 = pl.BlockSpec((PEER_HEADS, i1_per, tm), lambda i, j: (0, j, i))
    sel_i2 = pl.BlockSpec((PEER_HEADS, PEER_NKEYS, tm), lambda i, j: (0, 0, i))
    return pl.pallas_call(
        _peer_kernel,
        grid=(t // tm, e // te),
        in_specs=[
            pl.BlockSpec((te, d), lambda i, j: (j, 0)),
            pl.BlockSpec((d, tm), lambda i, j: (0, i)),
            pl.BlockSpec((d, te), lambda i, j: (0, j)),
            sel_i1,
            sel_i1,
            sel_i2,
            sel_i2,
        ],
        out_specs=pl.BlockSpec((tm, d), lambda i, j: (i, 0)),
        out_shape=jax.ShapeDtypeStruct((t, d), F32),
        scratch_shapes=[pltpu.VMEM((d, tm), F32), pltpu.VMEM((te, tm), BF16)],
        compiler_params=_params("parallel", "arbitrary"),
        name="peer_dense",
    )(u, xt, vt, na, ea, r2, eb)


def _final_kernel(h_ref, po_ref, p_ref, g3_ref, wg_ref, wp_ref, y_ref):
    h = h_ref[...] + po_ref[...]
    n3 = _rms(h, g3_ref[...])
    gate = jax.nn.sigmoid(_dot(n3.astype(BF16), wg_ref[...]))
    y_ref[...] = h + gate * _dot(p_ref[...].astype(BF16), wp_ref[...])


def _final(h, po, p, g3, wg, wp):
    t, d = h.shape
    tm = min(t, ROW_TILE)
    tok = lambda n: pl.BlockSpec((tm, n), lambda i: (i, 0))
    full = lambda a: pl.BlockSpec(a.shape, lambda i: (0,) * a.ndim)
    weights = [g3.reshape(1, -1), wg.astype(BF16), wp.astype(BF16)]
    return pl.pallas_call(
        _final_kernel,
        grid=(t // tm,),
        in_specs=[tok(d), tok(d), tok(p.shape[1])] + [full(a) for a in weights],
        out_specs=tok(d),
        out_shape=jax.ShapeDtypeStruct((t, d), F32),
        compiler_params=_params("parallel"),
        name="final",
    )(h, po, p, *weights)


def kernel(x_prompt, x_sample, cache_k, cache_v, p_prompt, p_sample, norm1_g, w_in, q_norm_g, k_norm_g, v_norm_g, spatial_w, spatial_b, out_norm_a_g, out_norm_b_g, w_out, norm2_g, peer_w_query, peer_sub_keys, peer_u, peer_v, norm3_g, ple_w_gate, ple_w_proj):
    depth = norm1_g.shape[0]
    bsz, s, d = x_prompt.shape
    db, dt, _ = x_sample.shape
    buf = cache_k.shape[2]
    assert bsz == 1 and dt == 1, "one prompt sequence and single-token decode rows"
    assert buf == MAX_WINDOW and s % (PATTERNS[-1][1] * BLK) == 0 and s >= MAX_WINDOW
    assert all(window == dil * BLK for window, dil in PATTERNS)
    assert PATTERNS[0][1] == 1
    hp, hs = x_prompt[0], x_sample[:, 0]
    outs = [[] for _ in range(5)]
    for l in range(depth):
        u_bf = peer_u[l].astype(BF16)
        vt_bf = peer_v[l].astype(BF16).T
        vg = v_norm_g[l].reshape(-1)

        def tail(h, atts, gm, p):
            h1, xt, st = _merge(h, atts, gm, out_norm_a_g[l], out_norm_b_g[l], w_out[l], norm2_g[l], peer_w_query[l], peer_sub_keys[l])
            po = _peer(xt, u_bf, vt_bf, *_select(st))
            return _final(h1, po, p, norm3_g[l], ple_w_gate[l], ple_w_proj[l])

        q, k, v, gm, *views = _proj(hp, 0, norm1_g[l], w_in[l], q_norm_g[l], k_norm_g[l], vg, spatial_w[l], spatial_b[l], False)
        per_pat = [_attn_prompt(q, k, v, 1)] + [_attn_prompt(*views[3 * i : 3 * i + 3], dil) for i, (_, dil) in enumerate(PATTERNS[1:])]
        hp = tail(hp, [o for o, _ in per_pat] + [lse for _, lse in per_pat], gm, p_prompt[l, 0])
        keep = min(MAX_WINDOW, s)
        outs[0].append(k[s - keep :].reshape(1, keep, A_HEADS, HEAD_DIM))
        outs[1].append(v[s - keep :].reshape(1, keep, A_HEADS, HEAD_DIM))

        q, k, v, gm, vn = _proj(hs, PAST_LEN, norm1_g[l], w_in[l], q_norm_g[l], k_norm_g[l], vg, spatial_w[l], spatial_b[l], True)
        att = _attn_sample(q, k, v, cache_k[l], cache_v[l])
        hs = tail(hs, [att], gm, p_sample[l, :, 0])
        for o, a in zip(outs[2:], (k, v, vn)):
            o.append(a.reshape(db, 1, A_HEADS, HEAD_DIM))
    return (hp[None], hs[:, None], *(jnp.stack(o) for o in outs))
```

```python
import functools

import jax
import jax.numpy as jnp
from jax import lax
from jax.experimental import pallas as pl
from jax.experimental.pallas import tpu as pltpu

HEAD_DIM = 64
A_HEADS = 8
A_WIDTH = A_HEADS * HEAD_DIM
B_GROUPS = 8
B_WIDTH = B_GROUPS * HEAD_DIM
PATTERNS = ((128, 1), (512, 4), (2048, 16))
MAX_WINDOW = 2048
PAST_LEN = 8192
BLK = 128
ROT_DIM = HEAD_DIM // 4
ROPE_THETA = 500000.0
PEER_HEADS = 8
PEER_NKEYS = 128
PEER_TOPK = 16
EPS = 1e-6
LANES = 128
BF16_SUBLANES = 16
ROW_TILE = 512
PEER_TOKEN_TILE = 1024
PEER_EXPERT_TILE = 1024
SELECT_TILE = 256
ATTN_ROWS = 512

VMEM_LIMIT = 48 * 1024 * 1024

F32 = jnp.float32
BF16 = jnp.bfloat16
NEG_INF = float("-inf")


def _params(*sem, flags=None):
    return pltpu.CompilerParams(dimension_semantics=sem, vmem_limit_bytes=VMEM_LIMIT, flags=flags)


def _dot(a, b):
    return jnp.dot(a, b, preferred_element_type=F32)


def _dot_split(a, m, passes):
    acc = None
    rem = a
    for _ in range(passes):
        piece = rem.astype(BF16)
        rem = rem - piece.astype(F32)
        t = _dot(piece, m)
        acc = t if acc is None else acc + t
    return acc


def _rms(x, g):
    return x * lax.rsqrt(jnp.mean(x * x, axis=-1, keepdims=True) + EPS) * g


def _gelu_tanh(x):
    c0 = 2.0 * (2.0 / jnp.pi) ** 0.5
    return x / (1.0 + jnp.exp(x * (-c0 - (c0 * 0.044715) * (x * x))))


def _tile4(t):
    return jnp.concatenate([t, t, t, t], axis=1)


def _proj_kernel(x_ref, g1_ref, win_ref, qg_ref, kg_ref, vg_ref, base_ref, cr_ref, sr_ref, ma_ref, mb_ref, mh_ref, *rest, sample):
    if sample:
        w00_ref, b0_ref, q_out, k_out, v_out, gm_out, vn_out = rest
    else:
        ws_ref, bias_ref, q_out, k_out, v_out, gm_out, *view_outs, slab_ref = rest
    tm = x_ref.shape[0]
    n1 = _rms(x_ref[...], g1_ref[...])
    z = _dot(n1.astype(BF16), win_ref[...])
    mh = mh_ref[...]
    cos_b, sin_b0 = base_ref[0, 0:1, :], base_ref[0, 1:2, :]
    cos_r, sin_r = cr_ref[...], sr_ref[...]
    cos_p = cos_b * cos_r - sin_b0 * sin_r
    sin_p = sin_b0 * cos_r + cos_b * sin_r
    cos_t, sin_a, sin_b = _tile4(cos_p), _tile4(sin_p * ma_ref[...]), _tile4(sin_p * mb_ref[...])

    def head_norm_rope(t, g):
        tn = t * lax.rsqrt(_dot_split(t * t, mh, 2) + EPS) * g
        return tn * cos_t + pltpu.roll(tn, A_WIDTH - ROT_DIM // 2, 1) * sin_a + pltpu.roll(tn, ROT_DIM // 2, 1) * sin_b

    q_out[...] = head_norm_rope(z[:, 0:A_WIDTH], qg_ref[...])
    k_out[...] = head_norm_rope(z[:, A_WIDTH : 2 * A_WIDTH], kg_ref[...])
    v_out[...] = z[:, 2 * A_WIDTH : 3 * A_WIDTH]
    u = jax.nn.gelu(z[:, 3 * A_WIDTH : 3 * A_WIDTH + B_WIDTH])
    gv = jax.nn.gelu(z[:, 3 * A_WIDTH + B_WIDTH :])
    vn = gv * lax.rsqrt(_dot_split(gv * gv, mh, 2) + EPS) * vg_ref[...]
    if sample:
        vn_out[...] = vn
        gm_out[...] = u * (vn * w00_ref[...] + b0_ref[...])
    else:
        row = lax.broadcasted_iota(jnp.int32, (BLK, BLK), 0)
        col = lax.broadcasted_iota(jnp.int32, (BLK, BLK), 1)
        lane_lo = col < HEAD_DIM
        causal = row >= col
        wm = [jnp.where(causal, ws_ref[g], 0.0).astype(BF16) for g in range(B_GROUPS)]
        vnb = vn.astype(BF16)
        for c in range(tm // BLK):
            pieces = []
            for j in range(B_WIDTH // LANES):
                s = vnb[c * BLK : (c + 1) * BLK, j * LANES : (j + 1) * LANES]
                pieces.append(jnp.where(lane_lo, _dot(wm[2 * j], s), _dot(wm[2 * j + 1], s)))
            mixed = jnp.concatenate(pieces, axis=1) + bias_ref[...]
            gm_out[c * BLK : (c + 1) * BLK, :] = u[c * BLK : (c + 1) * BLK, :] * mixed
        n_col = A_WIDTH // LANES
        for ai, src in enumerate((q_out, k_out, v_out)):
            for j in range(n_col):
                slab_ref[j] = src[:, j * LANES : (j + 1) * LANES]
            for vi, (_, dil) in enumerate(PATTERNS[1:]):
                dst = view_outs[3 * vi + ai]
                for r in range(dil):
                    for j in range(n_col):
                        c0 = r * A_WIDTH + j * LANES
                        dst[:, c0 : c0 + LANES] = slab_ref[j, pl.ds(r, tm // dil, stride=dil), :].astype(BF16)


def _proj(x, pos0, g1, w_in, qg, kg, vg, ws, bs, sample):
    t, d = x.shape
    tm = min(t, ROW_TILE)
    nt = t // tm
    half = ROT_DIM // 2
    inv = ROPE_THETA ** (-jnp.arange(0, ROT_DIM, 2, dtype=F32) / ROT_DIM)
    lane = jnp.arange(LANES) % HEAD_DIM
    freq = jnp.where(lane < ROT_DIM, inv[lane % half], 0.0)
    ma = jnp.where(lane < half, -1.0, 0.0).reshape(1, -1)
    mb = jnp.where((lane >= half) & (lane < ROT_DIM), 1.0, 0.0).reshape(1, -1)
    step = 0 if sample else tm
    ang_base = (pos0 + step * jnp.arange(nt)).astype(F32)[:, None] * freq[None, :]
    base = jnp.zeros((nt, 8, LANES), F32).at[:, 0].set(jnp.cos(ang_base)).at[:, 1].set(jnp.sin(ang_base))
    ang_row = (jnp.zeros((tm,), F32) if sample else jnp.arange(tm, dtype=F32))[:, None] * freq[None, :]
    head_id = jnp.arange(A_WIDTH) // HEAD_DIM
    mh = ((head_id[:, None] == head_id[None, :]).astype(F32) / HEAD_DIM).astype(BF16)
    row = lambda a: a.reshape(1, -1)
    tiled = lambda a: jnp.tile(a, A_HEADS).reshape(1, -1)
    full = lambda a: pl.BlockSpec(a.shape, lambda i: (0,) * a.ndim)
    tok = lambda n: pl.BlockSpec((tm, n), lambda i: (i, 0))
    args = [x, row(g1), w_in.astype(BF16), tiled(qg), tiled(kg), row(vg), base, jnp.cos(ang_row), jnp.sin(ang_row), ma, mb, mh]
    specs = [tok(d)] + [full(a) for a in args[1:6]] + [pl.BlockSpec((1, 8, LANES), lambda i: (i, 0, 0))] + [full(a) for a in args[7:]]
    if sample:
        extra = [jnp.repeat(ws[:, 0, 0], HEAD_DIM).reshape(1, -1), jnp.repeat(bs[:, 0], HEAD_DIM).reshape(1, -1)]
        out_specs = [tok(A_WIDTH)] * 5
        out_shape = [jax.ShapeDtypeStruct((t, A_WIDTH), F32)] * 5
    else:
        extra = [ws, jnp.repeat(bs.T, HEAD_DIM, axis=1)]
        out_specs = [tok(A_WIDTH)] * 4
        out_shape = [jax.ShapeDtypeStruct((t, A_WIDTH), F32)] * 4
        for _, dil in PATTERNS[1:]:
            out_specs += [pl.BlockSpec((tm // dil, dil * A_WIDTH), lambda i: (i, 0))] * 3
            out_shape += [jax.ShapeDtypeStruct((t // dil, dil * A_WIDTH), BF16)] * 3
    args += extra
    specs += [full(a) for a in extra]
    return pl.pallas_call(
        functools.partial(_proj_kernel, sample=sample),
        grid=(nt,),
        in_specs=specs,
        out_specs=out_specs,
        out_shape=out_shape,
        scratch_shapes=[] if sample else [pltpu.VMEM((A_WIDTH // LANES, tm, LANES), F32)],
        compiler_params=_params("parallel"),
        name="proj_sample" if sample else "proj_prompt",
    )(*args)


def _attn_kernel(q_ref, k_ref, kp_ref, v_ref, vp_ref, o_ref, l_ref, *, nsub):
    first = pl.program_id(1) == 0
    qi = lax.broadcasted_iota(jnp.int32, (BLK, 2 * BLK), 0)
    kj = lax.broadcasted_iota(jnp.int32, (BLK, 2 * BLK), 1)
    dist = BLK + qi - kj
    band = (dist >= 0) & (dist <= BLK)
    band_first = band & ((kj >= BLK) | jnp.logical_not(first))
    lane_lo = lax.broadcasted_iota(jnp.int32, (BLK, LANES), 1) < HEAD_DIM
    scale = HEAD_DIM**-0.5
    for sb in range(nsub):
        r0 = sb * BLK
        rows = slice(r0, r0 + BLK)
        prev = slice(r0 - BLK, r0)
        valid = band_first if sb == 0 else band
        for j in range(A_WIDTH // LANES):
            cols = slice(j * LANES, (j + 1) * LANES)
            qf = q_ref[rows, cols]
            kprev = kp_ref[:, cols] if sb == 0 else k_ref[prev, cols]
            vprev = vp_ref[:, cols] if sb == 0 else v_ref[prev, cols]
            k2 = jnp.concatenate([kprev, k_ref[rows, cols]], axis=0).astype(BF16)
            v2 = jnp.concatenate([vprev, v_ref[rows, cols]], axis=0).astype(BF16)
            halves = []
            for lo in (True, False):
                qm = jnp.where(lane_lo == lo, qf, jnp.zeros((), qf.dtype)).astype(BF16)
                s = lax.dot_general(qm, k2, (((1,), (1,)), ((), ())), preferred_element_type=F32) * scale
                s = jnp.where(valid, s, NEG_INF)
                m = jnp.max(s, axis=1, keepdims=True)
                e = jnp.exp(s - m)
                den = jnp.sum(e, axis=1, keepdims=True)
                halves.append((_dot(e.astype(BF16), v2) / den, m + jnp.log(den)))
            o_ref[rows, cols] = jnp.where(lane_lo, halves[0][0], halves[1][0]).astype(o_ref.dtype)
            l_ref[rows, cols] = jnp.where(lane_lo, halves[0][1], halves[1][1])


def _attn_prompt(qv, kv, vv, dil):
    length = qv.shape[0]
    qb = min(length, ATTN_ROWS)
    nsub = qb // BLK
    own = pl.BlockSpec((qb, A_WIDTH), lambda r, b: (b, r))
    prev = pl.BlockSpec((BLK, A_WIDTH), lambda r, b: (jnp.maximum(b * nsub - 1, 0), r))
    return pl.pallas_call(
        functools.partial(_attn_kernel, nsub=nsub),
        grid=(dil, length // qb),
        in_specs=[own, own, prev, own, prev],
        out_specs=[own, own],
        out_shape=[jax.ShapeDtypeStruct((length, dil * A_WIDTH), dt) for dt in (BF16, F32)],
        compiler_params=_params("parallel", "arbitrary"),
        name=f"attn_d{dil}",
    )(qv, kv, kv, vv, vv)


def _attn_sample_kernel(q_ref, ks_ref, vs_ref, k_ref, v_ref, w_ref, o_ref):
    w = w_ref[...]
    live = w > 0.0
    q8 = q_ref[0] * HEAD_DIM**-0.5
    q_cols, v_cols = q8.T, vs_ref[0].T
    s_new = jnp.sum(ks_ref[0] * q8, axis=1, keepdims=True)
    outs = []
    for h in range(A_HEADS):
        q = q_cols[:, h : h + 1]
        s = jnp.where(live, jnp.sum(k_ref[0, h] * q, axis=0, keepdims=True), NEG_INF)
        s_self = s_new[h : h + 1, :]
        m = jnp.maximum(jnp.max(s, axis=1, keepdims=True), s_self)
        e = w * jnp.exp(s - m)
        e_self = len(PATTERNS) * jnp.exp(s_self - m)
        den = jnp.sum(e, axis=1, keepdims=True) + e_self
        acc = jnp.sum(v_ref[0, h] * e, axis=1, keepdims=True) + e_self * v_cols[:, h : h + 1]
        outs.append(acc / den)
    o_ref[0] = jnp.concatenate(outs, axis=1).T


def _attn_sample(q, k, v, ck, cv):
    b, buf = ck.shape[0], ck.shape[1]
    mult = [0.0] * buf
    for window, dil in PATTERNS:
        for j in range(1, window // dil + 1):
            if buf - dil * j >= 0:
                mult[buf - dil * j] += 1.0
    w = jnp.asarray(mult, F32).reshape(1, buf)
    tok = pl.BlockSpec((1, A_HEADS, HEAD_DIM), lambda i: (i, 0, 0))
    cache = pl.BlockSpec((1, A_HEADS, HEAD_DIM, buf), lambda i: (i, 0, 0, 0))
    heads = lambda a: a.reshape(b, A_HEADS, HEAD_DIM)
    pos_minor = lambda a: jnp.transpose(a, (0, 2, 3, 1))
    out = pl.pallas_call(
        _attn_sample_kernel,
        grid=(b,),
        in_specs=[tok, tok, tok, cache, cache, pl.BlockSpec((1, buf), lambda i: (0, 0))],
        out_specs=tok,
        out_shape=jax.ShapeDtypeStruct((b, A_HEADS, HEAD_DIM), F32),
        compiler_params=_params("parallel"),
        name="attn_sample",
    )(heads(q), heads(k), heads(v), pos_minor(ck), pos_minor(cv), w)
    return out.reshape(b, A_WIDTH)


def _merge_kernel(*refs, n_pat):
    x_ref = refs[0]
    att_refs = refs[1 : 1 + 2 * n_pat] if n_pat else refs[1:2]
    gm_ref, ga_ref, gb_ref, wout_ref, g2_ref, wq_ref, sk_ref, h_out, xt_out, st_out = refs[1 + len(att_refs) :][:10]
    scratch = refs[11 + len(att_refs) :]
    tm = x_ref.shape[0]
    if n_pat:
        vals = []
        for idx, ref in enumerate(att_refs):
            dil = PATTERNS[idx % n_pat][1]
            if dil == 1:
                vals.append(ref[...].astype(F32))
                continue
            nat = scratch[idx]
            n_col = A_WIDTH // LANES
            for r in range(dil):
                for j in range(n_col):
                    c0 = r * A_WIDTH + j * LANES
                    nat[j, pl.ds(r, tm // dil, stride=dil), :] = ref[:, c0 : c0 + LANES].astype(F32)
            vals.append(jnp.concatenate([nat[j] for j in range(n_col)], axis=1))
        outs, lses = vals[:n_pat], vals[n_pat:]
        m = functools.reduce(jnp.maximum, lses)
        ws = [jnp.exp(l - m) for l in lses]
        att = sum(w * o for w, o in zip(ws, outs)) / sum(ws)
    else:
        att = att_refs[0][...]
    cat = jnp.concatenate([_rms(att, ga_ref[...]), _rms(gm_ref[...], gb_ref[...])], axis=1)
    h = x_ref[...] + _dot(cat.astype(BF16), wout_ref[...])
    h_out[...] = h
    n2 = _rms(h, g2_ref[...])
    xt_out[...] = n2.T.astype(BF16)
    qt = _dot(n2.astype(BF16), wq_ref[...]).T.astype(BF16)
    for i in range(2 * PEER_HEADS):
        rows = slice(i * PEER_NKEYS, (i + 1) * PEER_NKEYS)
        st_out[rows, :] = _dot(sk_ref[i], qt[rows, :])


def _merge(x, atts, gm, ga, gb, w_out, g2, wq, sub_keys):
    t, d = x.shape
    tm = min(t, ROW_TILE)
    n_pat = len(atts) // 2
    nq = wq.shape[1]
    sk = sub_keys.reshape(2 * PEER_HEADS, PEER_NKEYS, -1).astype(BF16)
    row = lambda a: a.reshape(1, -1)
    full = lambda a: pl.BlockSpec(a.shape, lambda i: (0,) * a.ndim)
    tok = lambda n: pl.BlockSpec((tm, n), lambda i: (i, 0))
    tokt = lambda n: pl.BlockSpec((n, tm), lambda i: (0, i))
    view = lambda a: pl.BlockSpec((tm * A_WIDTH // a.shape[1], a.shape[1]), lambda i: (i, 0))
    weights = [row(ga), row(gb), w_out.astype(BF16), row(g2), wq.astype(BF16), sk]
    return pl.pallas_call(
        functools.partial(_merge_kernel, n_pat=n_pat),
        grid=(t // tm,),
        in_specs=[tok(d)] + [view(a) for a in atts] + [tok(A_WIDTH)] + [full(a) for a in weights],
        out_specs=[tok(d), tokt(d), tokt(nq)],
        out_shape=[
            jax.ShapeDtypeStruct((t, d), F32),
            jax.ShapeDtypeStruct((d, t), BF16),
            jax.ShapeDtypeStruct((nq, t), F32),
        ],
        scratch_shapes=[pltpu.VMEM((A_WIDTH // LANES, tm, LANES), F32)] * (2 * n_pat),
        compiler_params=_params("parallel"),
        name="merge_prompt" if n_pat else "merge_sample",
    )(x, *atts, gm, *weights)


def _merge_exchange(n):
    pairs, t = [], n.bit_length() - 1
    p = 1 << (t - 1)
    while p:
        q, r, d = 1 << (t - 1), 0, p
        while d:
            pairs += [(i, i + d) for i in range(n - d) if (i & p) == r]
            d, q, r = q - p, q // 2, p
        p //= 2
    return pairs


_SORT_TOPK = _merge_exchange(PEER_TOPK)


def _exchange(x, i, j):
    x[i], x[j] = jnp.maximum(x[i], x[j]), jnp.minimum(x[i], x[j])


def _sublane_all(op, x):
    for shift in (4, 2, 1):
        x = op(x, pltpu.roll(x, shift, 0))
    return x


def _sorted_top16(vals):
    x = list(vals)
    for i, j in _SORT_TOPK:
        _exchange(x, i, j)
    for shift in (4, 2, 1):
        y = [pltpu.roll(v, shift, 0) for v in x]
        x = [jnp.maximum(x[r], y[PEER_TOPK - 1 - r]) for r in range(PEER_TOPK)]
        stride = PEER_TOPK // 2
        while stride:
            for i in range(PEER_TOPK):
                if not i & stride:
                    _exchange(x, i, i + stride)
            stride //= 2
    return x


_CAND_ROWS = ((0, 0, 8), (0, 8, 8), (1, 0, 8), (2, 0, 5), (3, 0, 4), (4, 0, 3), (5, 0, 2), (6, 0, 2), (7, 0, 2))


def _select_fast(a_vals, b_vals):
    k = PEER_TOPK
    one = lambda cond: jnp.where(cond, 1.0, 0.0)
    ta, tb = _sorted_top16(a_vals), _sorted_top16(b_vals)
    sub = lax.broadcasted_iota(jnp.int32, ta[0].shape, 0)
    spread = lambda t, off: functools.reduce(lambda acc, s: jnp.where(sub == s, t[off + s], acc), range(1, 8), t[off])
    tb_lo, tb_hi, ta_hi = spread(tb, 0), spread(tb, 8), spread(ta, 8)
    cands = []
    for k1, k2_0, n_k2 in _CAND_ROWS:
        c = ta[k1] + (tb_hi if k2_0 else tb_lo)
        cands.append(c if n_k2 == 8 else jnp.where(sub < n_k2, c, NEG_INF))
    cands.append(ta_hi + tb[0])
    rem = list(cands)
    for _ in range(k):
        m = _sublane_all(jnp.maximum, functools.reduce(jnp.maximum, rem))
        rem = [jnp.where(c == m, NEG_INF, c) for c in rem]
    chosen = [one(r != c) for r, c in zip(rem, cands)]
    n_chosen = _sublane_all(jnp.add, functools.reduce(jnp.add, chosen))
    top = ta[0] + tb[0]
    z = _sublane_all(jnp.add, functools.reduce(jnp.add, [ch * jnp.exp(c - top) for ch, c in zip(chosen, cands)]))
    n_k1 = [_sublane_all(jnp.add, chosen[0] + chosen[1])] + [_sublane_all(jnp.add, chosen[i]) for i in range(2, 9)]
    n_k1 += [_sublane_all(jnp.add, jnp.where(sub == s, chosen[9], 0.0)) for s in range(8)]
    inv_z = 1.0 / z
    n_at, ea, r2, eb = [], [], [], []
    cnt_a = cnt_b = None
    for av, bv in zip(a_vals, b_vals):
        n = jnp.zeros_like(av)
        for r in range(k):
            n = jnp.where(av == ta[r], n_k1[r], n)
        n_at.append(n)
        ea.append(jnp.exp(av - ta[0]) * inv_z)
        rank = jnp.zeros_like(bv)
        for r in range(k):
            rank = jnp.where(tb[r] > bv, r + 1.0, rank)
        r2.append(rank)
        eb.append(jnp.exp(bv - tb[0]))
        ca, cb = one(av >= ta[k - 1]), one(bv >= tb[k - 1])
        cnt_a, cnt_b = (ca, cb) if cnt_a is None else (cnt_a + ca, cnt_b + cb)
    flag = one(n_chosen != k) + one(_sublane_all(jnp.add, cnt_a) != k) + one(_sublane_all(jnp.add, cnt_b) != k)
    for r in range(k - 1):
        flag = flag + one(ta[r] == ta[r + 1]) + one(tb[r] == tb[r + 1])
    return n_at, ea, r2, eb, flag


def _top16(a, iota_k):
    rem = a
    rank = jnp.full(a.shape, PEER_TOPK, jnp.int32)
    vals = []
    for r in range(PEER_TOPK):
        m = jnp.max(rem, axis=0, keepdims=True)
        first = jnp.min(jnp.where(rem == m, iota_k, PEER_NKEYS), axis=0, keepdims=True)
        sel = iota_k == first
        rank = jnp.where(sel, r, rank)
        rem = jnp.where(sel, NEG_INF, rem)
        vals.append(m)
    return jnp.concatenate(vals, axis=0), rank


def _select_exact(a, b):
    iota_k = lax.broadcasted_iota(jnp.int32, a.shape, 0)
    ta, r1 = _top16(a, iota_k)
    tb, r2 = _top16(b, iota_k)
    cand = jnp.concatenate([ta[k : k + 1, :] + tb for k in range(PEER_TOPK)], axis=0)
    pos = lax.broadcasted_iota(jnp.int32, cand.shape, 0)
    rem = cand
    chosen = jnp.zeros(cand.shape, F32)
    for _ in range(PEER_TOPK):
        m = jnp.max(rem, axis=0, keepdims=True)
        first = jnp.min(jnp.where(rem == m, pos, cand.shape[0]), axis=0, keepdims=True)
        sel = pos == first
        chosen = jnp.where(sel, 1.0, chosen)
        rem = jnp.where(sel, NEG_INF, rem)
    z = jnp.sum(chosen * jnp.exp(cand - cand[0:1, :]), axis=0, keepdims=True)
    n_at = jnp.zeros(a.shape, F32)
    for k in range(PEER_TOPK):
        n_k = jnp.sum(chosen[k * PEER_TOPK : (k + 1) * PEER_TOPK, :], axis=0, keepdims=True)
        n_at = jnp.where(r1 == k, n_k, n_at)
    return n_at, jnp.exp(a - ta[0:1, :]) / z, r2.astype(F32), jnp.exp(b - tb[0:1, :])


def _select_kernel(st_ref, na_ref, ea_ref, r2_ref, eb_ref):
    tk = st_ref.shape[1]
    n_tile = PEER_NKEYS // 8

    def head(h, carry):
        base = pl.multiple_of(h * (2 * PEER_NKEYS), 2 * PEER_NKEYS)
        for lt in range(tk // LANES):
            lanes = slice(lt * LANES, (lt + 1) * LANES)
            a_vals = [st_ref[pl.ds(base + 8 * v, 8), lanes] for v in range(n_tile)]
            b_vals = [st_ref[pl.ds(base + PEER_NKEYS + 8 * v, 8), lanes] for v in range(n_tile)]
            n_at, ea, r2, eb, flag = _select_fast(a_vals, b_vals)
            stack = lambda tiles: jnp.concatenate(tiles, axis=0)
            na_ref[h, :, lanes] = stack(n_at)
            ea_ref[h, :, lanes] = stack(ea)
            r2_ref[h, :, lanes] = stack(r2).astype(BF16)
            eb_ref[h, :, lanes] = stack(eb).astype(BF16)

            @pl.when(jnp.max(flag) > 0.0)
            def _():
                n_x, ea_x, r2_x, eb_x = _select_exact(st_ref[pl.ds(base, PEER_NKEYS), lanes], st_ref[pl.ds(base + PEER_NKEYS, PEER_NKEYS), lanes])
                na_ref[h, :, lanes] = n_x
                ea_ref[h, :, lanes] = ea_x
                r2_ref[h, :, lanes] = r2_x.astype(BF16)
                eb_ref[h, :, lanes] = eb_x.astype(BF16)

        return carry

    lax.fori_loop(0, PEER_HEADS, head, 0)


def _select(st):
    t = st.shape[1]
    tk = min(t, SELECT_TILE)
    out = pl.BlockSpec((PEER_HEADS, PEER_NKEYS, tk), lambda i: (0, 0, i))
    return pl.pallas_call(
        _select_kernel,
        grid=(t // tk,),
        in_specs=[pl.BlockSpec((2 * PEER_HEADS * PEER_NKEYS, tk), lambda i: (0, i))],
        out_specs=[out] * 4,
        out_shape=[jax.ShapeDtypeStruct((PEER_HEADS, PEER_NKEYS, t), dt) for dt in (F32, F32, BF16, BF16)],
        compiler_params=_params("parallel"),
        name="peer_select",
    )(st)


def _peer_kernel(u_ref, xt_ref, vt_ref, na_ref, ea_ref, r2_ref, eb_ref, o_ref, acc_ref, gt_ref):
    j = pl.program_id(1)

    @pl.when(j == 0)
    def _():
        acc_ref[...] = jnp.zeros_like(acc_ref)

    tm = xt_ref.shape[1]
    sub = BF16_SUBLANES
    row_tile = lambda ref, h, ii: jnp.broadcast_to(ref[h, ii : ii + 1, :], (sub, tm)).astype(BF16)[None]
    tiles = lambda a: a.reshape(PEER_NKEYS // sub, sub, tm)
    st = _dot(u_ref[...], xt_ref[...])
    for ii in range(u_ref.shape[0] // PEER_NKEYS):
        w = None
        for h in range(PEER_HEADS):
            hit = tiles(r2_ref[h]) < row_tile(na_ref, h, ii)
            t = jnp.where(hit, tiles(eb_ref[h]), jnp.zeros((), BF16)) * row_tile(ea_ref, h, ii)
            w = t if w is None else w + t
        rows = slice(ii * PEER_NKEYS, (ii + 1) * PEER_NKEYS)
        g = tiles(_gelu_tanh(st[rows, :]).astype(BF16)) * w
        gt_ref[rows, :] = g.reshape(PEER_NKEYS, tm)
    acc_ref[...] += _dot(vt_ref[...], gt_ref[...])

    @pl.when(j == pl.num_programs(1) - 1)
    def _():
        o_ref[...] = acc_ref[...].T


def _peer(xt, u, vt, na, ea, r2, eb):
    d, t = xt.shape
    e = u.shape[0]
    tm = min(t, PEER_TOKEN_TILE)
    te = PEER_EXPERT_TILE
    i1_per = te // PEER_NKEYS
    sel_i1 = pl.BlockSpec((PEER_HEADS, i1_per, tm), lambda i, j: (0, j, i))
    sel_i2 = pl.BlockSpec((PEER_HEADS, PEER_NKEYS, tm), lambda i, j: (0, 0, i))
    return pl.pallas_call(
        _peer_kernel,
        grid=(t // tm, e // te),
        in_specs=[
            pl.BlockSpec((te, d), lambda i, j: (j, 0)),
            pl.BlockSpec((d, tm), lambda i, j: (0, i)),
            pl.BlockSpec((d, te), lambda i, j: (0, j)),
            sel_i1,
            sel_i1,
            sel_i2,
            sel_i2,
        ],
        out_specs=pl.BlockSpec((tm, d), lambda i, j: (i, 0)),
        out_shape=jax.ShapeDtypeStruct((t, d), F32),
        scratch_shapes=[pltpu.VMEM((d, tm), F32), pltpu.VMEM((te, tm), BF16)],
        compiler_params=_params("parallel", "arbitrary"),
        name="peer_dense",
    )(u, xt, vt, na, ea, r2, eb)


def _final_kernel(h_ref, po_ref, p_ref, g3_ref, wg_ref, wp_ref, y_ref):
    h = h_ref[...] + po_ref[...]
    n3 = _rms(h, g3_ref[...])
    gate = jax.nn.sigmoid(_dot(n3.astype(BF16), wg_ref[...]))
    y_ref[...] = h + gate * _dot(p_ref[...].astype(BF16), wp_ref[...])


def _final(h, po, p, g3, wg, wp):
    t, d = h.shape
    tm = min(t, ROW_TILE)
    tok = lambda n: pl.BlockSpec((tm, n), lambda i: (i, 0))
    full = lambda a: pl.BlockSpec(a.shape, lambda i: (0,) * a.ndim)
    weights = [g3.reshape(1, -1), wg.astype(BF16), wp.astype(BF16)]
    return pl.pallas_call(
        _final_kernel,
        grid=(t // tm,),
        in_specs=[tok(d), tok(d), tok(p.shape[1])] + [full(a) for a in weights],
        out_specs=tok(d),
        out_shape=jax.ShapeDtypeStruct((t, d), F32),
        compiler_params=_params("parallel"),
        name="final",
    )(h, po, p, *weights)


def kernel(x_prompt, x_sample, cache_k, cache_v, p_prompt, p_sample, norm1_g, w_in, q_norm_g, k_norm_g, v_norm_g, spatial_w, spatial_b, out_norm_a_g, out_norm_b_g, w_out, norm2_g, peer_w_query, peer_sub_keys, peer_u, peer_v, norm3_g, ple_w_gate, ple_w_proj):
    depth = norm1_g.shape[0]
    bsz, s, d = x_prompt.shape
    db, dt, _ = x_sample.shape
    buf = cache_k.shape[2]
    assert bsz == 1 and dt == 1, "one prompt sequence and single-token decode rows"
    assert buf == MAX_WINDOW and s % (PATTERNS[-1][1] * BLK) == 0 and s >= MAX_WINDOW
    assert all(window == dil * BLK for window, dil in PATTERNS)
    assert PATTERNS[0][1] == 1
    hp, hs = x_prompt[0], x_sample[:, 0]
    outs = [[] for _ in range(5)]
    for l in range(depth):
        u_bf = peer_u[l].astype(BF16)
        vt_bf = peer_v[l].astype(BF16).T
        vg = v_norm_g[l].reshape(-1)

        def tail(h, atts, gm, p):
            h1, xt, st = _merge(h, atts, gm, out_norm_a_g[l], out_norm_b_g[l], w_out[l], norm2_g[l], peer_w_query[l], peer_sub_keys[l])
            po = _peer(xt, u_bf, vt_bf, *_select(st))
            return _final(h1, po, p, norm3_g[l], ple_w_gate[l], ple_w_proj[l])

        q, k, v, gm, *views = _proj(hp, 0, norm1_g[l], w_in[l], q_norm_g[l], k_norm_g[l], vg, spatial_w[l], spatial_b[l], False)
        per_pat = [_attn_prompt(q, k, v, 1)] + [_attn_prompt(*views[3 * i : 3 * i + 3], dil) for i, (_, dil) in enumerate(PATTERNS[1:])]
        hp = tail(hp, [o for o, _ in per_pat] + [lse for _, lse in per_pat], gm, p_prompt[l, 0])
        keep = min(MAX_WINDOW, s)
        outs[0].append(k[s - keep :].reshape(1, keep, A_HEADS, HEAD_DIM))
        outs[1].append(v[s - keep :].reshape(1, keep, A_HEADS, HEAD_DIM))

        q, k, v, gm, vn = _proj(hs, PAST_LEN, norm1_g[l], w_in[l], q_norm_g[l], k_norm_g[l], vg, spatial_w[l], spatial_b[l], True)
        att = _attn_sample(q, k, v, cache_k[l], cache_v[l])
        hs = tail(hs, [att], gm, p_sample[l, :, 0])
        for o, a in zip(outs[2:], (k, v, vn)):
            o.append(a.reshape(db, 1, A_HEADS, HEAD_DIM))
    return (hp[None], hs[:, None], *(jnp.stack(o) for o in outs))
```

```python
import functools
import math
import struct

import jax
import jax.numpy as jnp
from jax import lax
from jax.experimental import pallas as pl
from jax.experimental.pallas import tpu as pltpu

HEAD_DIM = 64
A_HEADS = 8
A_WIDTH = A_HEADS * HEAD_DIM
B_GROUPS = 8
B_WIDTH = B_GROUPS * HEAD_DIM
PATTERNS = ((128, 1), (512, 4), (2048, 16))
MAX_WINDOW = 2048
PAST_LEN = 8192
BLK = 128
ROT_DIM = HEAD_DIM // 4
ROPE_THETA = 500000.0
PEER_HEADS = 8
PEER_NKEYS = 128
PEER_TOPK = 16
EPS = 1e-6
LANES = 128
BF16_SUBLANES = 16
ROW_TILE = 512

VMEM_LIMIT = 48 * 1024 * 1024

F32 = jnp.float32
BF16 = jnp.bfloat16
NEG_INF = float("-inf")


def _params(*sem, flags=None):
    return pltpu.CompilerParams(dimension_semantics=sem, vmem_limit_bytes=VMEM_LIMIT, flags=flags)


def _dot(a, b):
    return jnp.dot(a, b, preferred_element_type=F32)


def _dot_split(a, m, passes):
    acc = None
    rem = a
    for _ in range(passes):
        piece = rem.astype(BF16)
        rem = rem - piece.astype(F32)
        t = _dot(piece, m)
        acc = t if acc is None else acc + t
    return acc


def _rms(x, g):
    return x * lax.rsqrt(jnp.mean(x * x, axis=-1, keepdims=True) + EPS) * g


def _gelu_tanh(x):
    c0 = 2.0 * (2.0 / jnp.pi) ** 0.5
    return x / (1.0 + jnp.exp(x * (-c0 - (c0 * 0.044715) * (x * x))))


def _bf16_round(c):
    bits = struct.unpack("<I", struct.pack("<f", c))[0]
    bits = (bits + 0x7FFF + ((bits >> 16) & 1)) & 0xFFFF0000
    return struct.unpack("<f", struct.pack("<I", bits))[0]


def _split_bf16(c):
    hi = _bf16_round(c)
    return hi, _bf16_round(c - hi)


_GELU_C0 = _split_bf16(2.0 * (2.0 / math.pi) ** 0.5)
_GELU_C1 = _split_bf16(2.0 * (2.0 / math.pi) ** 0.5 * 0.044715)


def _gelu_tanh_packed(x):
    t2 = x * x
    poly = ((t2 * (-_GELU_C1[0]) + t2 * (-_GELU_C1[1])) - _GELU_C0[0]) - _GELU_C0[1]
    return x / (1.0 + jnp.exp(x * poly))


def _tile4(t):
    return jnp.concatenate([t, t, t, t], axis=1)


def _proj_kernel(x_ref, g1_ref, win_ref, qg_ref, kg_ref, vg_ref, base_ref, cr_ref, sr_ref, ma_ref, mb_ref, mh_ref, *rest, sample):
    if sample:
        w00_ref, b0_ref, q_out, k_out, v_out, gm_out, vn_out = rest
    else:
        ws_ref, bias_ref, q_out, k_out, v_out, gm_out, *view_outs, slab_ref = rest
    tm = x_ref.shape[0]
    n1 = _rms(x_ref[...], g1_ref[...])
    z = _dot(n1.astype(BF16), win_ref[...])
    mh = mh_ref[...]
    cos_b, sin_b0 = base_ref[0, 0:1, :], base_ref[0, 1:2, :]
    cos_r, sin_r = cr_ref[...], sr_ref[...]
    cos_p = cos_b * cos_r - sin_b0 * sin_r
    sin_p = sin_b0 * cos_r + cos_b * sin_r
    cos_t, sin_a, sin_b = _tile4(cos_p), _tile4(sin_p * ma_ref[...]), _tile4(sin_p * mb_ref[...])

    def head_norm_rope(t, g):
        tn = t * lax.rsqrt(_dot_split(t * t, mh, 2) + EPS) * g
        return tn * cos_t + pltpu.roll(tn, A_WIDTH - ROT_DIM // 2, 1) * sin_a + pltpu.roll(tn, ROT_DIM // 2, 1) * sin_b

    q_out[...] = head_norm_rope(z[:, 0:A_WIDTH], qg_ref[...])
    k_out[...] = head_norm_rope(z[:, A_WIDTH : 2 * A_WIDTH], kg_ref[...])
    v_out[...] = z[:, 2 * A_WIDTH : 3 * A_WIDTH]
    u = jax.nn.gelu(z[:, 3 * A_WIDTH : 3 * A_WIDTH + B_WIDTH])
    gv = jax.nn.gelu(z[:, 3 * A_WIDTH + B_WIDTH :])
    vn = gv * lax.rsqrt(_dot_split(gv * gv, mh, 2) + EPS) * vg_ref[...]
    if sample:
        vn_out[...] = vn
        gm_out[...] = u * (vn * w00_ref[...] + b0_ref[...])
    else:
        row = lax.broadcasted_iota(jnp.int32, (BLK, BLK), 0)
        col = lax.broadcasted_iota(jnp.int32, (BLK, BLK), 1)
        lane_lo = col < HEAD_DIM
        causal = row >= col
        wm = [jnp.where(causal, ws_ref[g], 0.0).astype(BF16) for g in range(B_GROUPS)]
        vnb = vn.astype(BF16)
        for c in range(tm // BLK):
            pieces = []
            for j in range(B_WIDTH // LANES):
                s = vnb[c * BLK : (c + 1) * BLK, j * LANES : (j + 1) * LANES]
                pieces.append(jnp.where(lane_lo, _dot(wm[2 * j], s), _dot(wm[2 * j + 1], s)))
            mixed = jnp.concatenate(pieces, axis=1) + bias_ref[...]
            gm_out[c * BLK : (c + 1) * BLK, :] = u[c * BLK : (c + 1) * BLK, :] * mixed
        n_col = A_WIDTH // LANES
        for ai, src in enumerate((q_out, k_out, v_out)):
            for j in range(n_col):
                slab_ref[j] = src[:, j * LANES : (j + 1) * LANES]
            for vi, (_, dil) in enumerate(PATTERNS[1:]):
                dst = view_outs[3 * vi + ai]
                for r in range(dil):
                    for j in range(n_col):
                        c0 = r * A_WIDTH + j * LANES
                        dst[:, c0 : c0 + LANES] = slab_ref[j, pl.ds(r, tm // dil, stride=dil), :].astype(BF16)


def _proj(x, pos0, g1, w_in, qg, kg, vg, ws, bs, sample):
    t, d = x.shape
    tm = min(t, ROW_TILE)
    nt = t // tm
    half = ROT_DIM // 2
    inv = ROPE_THETA ** (-jnp.arange(0, ROT_DIM, 2, dtype=F32) / ROT_DIM)
    lane = jnp.arange(LANES) % HEAD_DIM
    freq = jnp.where(lane < ROT_DIM, inv[lane % half], 0.0)
    ma = jnp.where(lane < half, -1.0, 0.0).reshape(1, -1)
    mb = jnp.where((lane >= half) & (lane < ROT_DIM), 1.0, 0.0).reshape(1, -1)
    step = 0 if sample else tm
    ang_base = (pos0 + step * jnp.arange(nt)).astype(F32)[:, None] * freq[None, :]
    base = jnp.zeros((nt, 8, LANES), F32).at[:, 0].set(jnp.cos(ang_base)).at[:, 1].set(jnp.sin(ang_base))
    ang_row = (jnp.zeros((tm,), F32) if sample else jnp.arange(tm, dtype=F32))[:, None] * freq[None, :]
    head_id = jnp.arange(A_WIDTH) // HEAD_DIM
    mh = ((head_id[:, None] == head_id[None, :]).astype(F32) / HEAD_DIM).astype(BF16)
    row = lambda a: a.reshape(1, -1)
    tiled = lambda a: jnp.tile(a, A_HEADS).reshape(1, -1)
    full = lambda a: pl.BlockSpec(a.shape, lambda i: (0,) * a.ndim)
    tok = lambda n: pl.BlockSpec((tm, n), lambda i: (i, 0))
    args = [x, row(g1), w_in.astype(BF16), tiled(qg), tiled(kg), row(vg), base, jnp.cos(ang_row), jnp.sin(ang_row), ma, mb, mh]
    specs = [tok(d)] + [full(a) for a in args[1:6]] + [pl.BlockSpec((1, 8, LANES), lambda i: (i, 0, 0))] + [full(a) for a in args[7:]]
    if sample:
        extra = [jnp.repeat(ws[:, 0, 0], HEAD_DIM).reshape(1, -1), jnp.repeat(bs[:, 0], HEAD_DIM).reshape(1, -1)]
        out_specs = [tok(A_WIDTH)] * 5
        out_shape = [jax.ShapeDtypeStruct((t, A_WIDTH), F32)] * 5
    else:
        extra = [ws, jnp.repeat(bs.T, HEAD_DIM, axis=1)]
        out_specs = [tok(A_WIDTH)] * 4
        out_shape = [jax.ShapeDtypeStruct((t, A_WIDTH), F32)] * 4
        for _, dil in PATTERNS[1:]:
            out_specs += [pl.BlockSpec((tm // dil, dil * A_WIDTH), lambda i: (i, 0))] * 3
            out_shape += [jax.ShapeDtypeStruct((t // dil, dil * A_WIDTH), BF16)] * 3
    args += extra
    specs += [full(a) for a in extra]
    return pl.pallas_call(
        functools.partial(_proj_kernel, sample=sample),
        grid=(nt,),
        in_specs=specs,
        out_specs=out_specs,
        out_shape=out_shape,
        scratch_shapes=[] if sample else [pltpu.VMEM((A_WIDTH // LANES, tm, LANES), F32)],
        compiler_params=_params("parallel"),
        name="proj_sample" if sample else "proj_prompt",
    )(*args)


def _attn_kernel(q_ref, k_ref, kp_ref, v_ref, vp_ref, o_ref, l_ref, *, nsub):
    first = pl.program_id(1) == 0
    qi = lax.broadcasted_iota(jnp.int32, (BLK, 2 * BLK), 0)
    kj = lax.broadcasted_iota(jnp.int32, (BLK, 2 * BLK), 1)
    dist = BLK + qi - kj
    band = (dist >= 0) & (dist <= BLK)
    band_first = band & ((kj >= BLK) | jnp.logical_not(first))
    lane_lo = lax.broadcasted_iota(jnp.int32, (BLK, LANES), 1) < HEAD_DIM
    scale = HEAD_DIM**-0.5
    for sb in range(nsub):
        r0 = sb * BLK
        rows = slice(r0, r0 + BLK)
        prev = slice(r0 - BLK, r0)
        valid = band_first if sb == 0 else band
        for j in range(A_WIDTH // LANES):
            cols = slice(j * LANES, (j + 1) * LANES)
            qf = q_ref[rows, cols]
            kprev = kp_ref[:, cols] if sb == 0 else k_ref[prev, cols]
            vprev = vp_ref[:, cols] if sb == 0 else v_ref[prev, cols]
            k2 = jnp.concatenate([kprev, k_ref[rows, cols]], axis=0).astype(BF16)
            v2 = jnp.concatenate([vprev, v_ref[rows, cols]], axis=0).astype(BF16)
            halves = []
            for lo in (True, False):
                qm = jnp.where(lane_lo == lo, qf, jnp.zeros((), qf.dtype)).astype(BF16)
                s = lax.dot_general(qm, k2, (((1,), (1,)), ((), ())), preferred_element_type=F32) * scale
                s = jnp.where(valid, s, NEG_INF)
                m = jnp.max(s, axis=1, keepdims=True)
                e = jnp.exp(s - m)
                den = jnp.sum(e, axis=1, keepdims=True)
                halves.append((_dot(e.astype(BF16), v2) / den, m + jnp.log(den)))
            o_ref[rows, cols] = jnp.where(lane_lo, halves[0][0], halves[1][0])
            l_ref[rows, cols] = jnp.where(lane_lo, halves[0][1], halves[1][1])


def _attn_prompt(qv, kv, vv, dil):
    length = qv.shape[0]
    qb = min(length, 512)
    nsub = qb // BLK
    own = pl.BlockSpec((qb, A_WIDTH), lambda r, b: (b, r))
    prev = pl.BlockSpec((BLK, A_WIDTH), lambda r, b: (jnp.maximum(b * nsub - 1, 0), r))
    return pl.pallas_call(
        functools.partial(_attn_kernel, nsub=nsub),
        grid=(dil, length // qb),
        in_specs=[own, own, prev, own, prev],
        out_specs=[own, own],
        out_shape=[jax.ShapeDtypeStruct((length, dil * A_WIDTH), F32)] * 2,
        compiler_params=_params("parallel", "arbitrary"),
        name=f"attn_d{dil}",
    )(qv, kv, kv, vv, vv)


def _attn_sample_kernel(q_ref, ks_ref, vs_ref, k_ref, v_ref, w_ref, o_ref):
    w = w_ref[...]
    live = w > 0.0
    q8 = q_ref[0] * HEAD_DIM**-0.5
    q_cols, v_cols = q8.T, vs_ref[0].T
    s_new = jnp.sum(ks_ref[0] * q8, axis=1, keepdims=True)
    outs = []
    for h in range(A_HEADS):
        q = q_cols[:, h : h + 1]
        s = jnp.where(live, jnp.sum(k_ref[0, h] * q, axis=0, keepdims=True), NEG_INF)
        s_self = s_new[h : h + 1, :]
        m = jnp.maximum(jnp.max(s, axis=1, keepdims=True), s_self)
        e = w * jnp.exp(s - m)
        e_self = len(PATTERNS) * jnp.exp(s_self - m)
        den = jnp.sum(e, axis=1, keepdims=True) + e_self
        acc = jnp.sum(v_ref[0, h] * e, axis=1, keepdims=True) + e_self * v_cols[:, h : h + 1]
        outs.append(acc / den)
    o_ref[0] = jnp.concatenate(outs, axis=1).T


def _attn_sample(q, k, v, ck, cv):
    b, buf = ck.shape[0], ck.shape[1]
    mult = [0.0] * buf
    for window, dil in PATTERNS:
        for j in range(1, window // dil + 1):
            if buf - dil * j >= 0:
                mult[buf - dil * j] += 1.0
    w = jnp.asarray(mult, F32).reshape(1, buf)
    tok = pl.BlockSpec((1, A_HEADS, HEAD_DIM), lambda i: (i, 0, 0))
    cache = pl.BlockSpec((1, A_HEADS, HEAD_DIM, buf), lambda i: (i, 0, 0, 0))
    heads = lambda a: a.reshape(b, A_HEADS, HEAD_DIM)
    pos_minor = lambda a: jnp.transpose(a, (0, 2, 3, 1))
    out = pl.pallas_call(
        _attn_sample_kernel,
        grid=(b,),
        in_specs=[tok, tok, tok, cache, cache, pl.BlockSpec((1, buf), lambda i: (0, 0))],
        out_specs=tok,
        out_shape=jax.ShapeDtypeStruct((b, A_HEADS, HEAD_DIM), F32),
        compiler_params=_params("parallel"),
        name="attn_sample",
    )(heads(q), heads(k), heads(v), pos_minor(ck), pos_minor(cv), w)
    return out.reshape(b, A_WIDTH)


def _merge_kernel(*refs, n_pat):
    x_ref = refs[0]
    att_refs = refs[1 : 1 + 2 * n_pat] if n_pat else refs[1:2]
    gm_ref, ga_ref, gb_ref, wout_ref, g2_ref, wq_ref, sk_ref, h_out, xt_out, st_out = refs[1 + len(att_refs) :][:10]
    scratch = refs[11 + len(att_refs) :]
    tm = x_ref.shape[0]
    if n_pat:
        vals = []
        for idx, ref in enumerate(att_refs):
            dil = PATTERNS[idx % n_pat][1]
            if dil == 1:
                vals.append(ref[...])
                continue
            nat = scratch[idx]
            n_col = A_WIDTH // LANES
            for r in range(dil):
                for j in range(n_col):
                    c0 = r * A_WIDTH + j * LANES
                    nat[j, pl.ds(r, tm // dil, stride=dil), :] = ref[:, c0 : c0 + LANES]
            vals.append(jnp.concatenate([nat[j] for j in range(n_col)], axis=1))
        outs, lses = vals[:n_pat], vals[n_pat:]
        m = functools.reduce(jnp.maximum, lses)
        ws = [jnp.exp(l - m) for l in lses]
        att = sum(w * o for w, o in zip(ws, outs)) / sum(ws)
    else:
        att = att_refs[0][...]
    cat = jnp.concatenate([_rms(att, ga_ref[...]), _rms(gm_ref[...], gb_ref[...])], axis=1)
    h = x_ref[...] + _dot(cat.astype(BF16), wout_ref[...])
    h_out[...] = h
    n2 = _rms(h, g2_ref[...])
    xt_out[...] = n2.T.astype(BF16)
    qt = _dot(n2.astype(BF16), wq_ref[...]).T.astype(BF16)
    for i in range(2 * PEER_HEADS):
        rows = slice(i * PEER_NKEYS, (i + 1) * PEER_NKEYS)
        st_out[rows, :] = _dot(sk_ref[i], qt[rows, :])


def _merge(x, atts, gm, ga, gb, w_out, g2, wq, sub_keys):
    t, d = x.shape
    tm = min(t, ROW_TILE)
    n_pat = len(atts) // 2
    nq = wq.shape[1]
    sk = sub_keys.reshape(2 * PEER_HEADS, PEER_NKEYS, -1).astype(BF16)
    row = lambda a: a.reshape(1, -1)
    full = lambda a: pl.BlockSpec(a.shape, lambda i: (0,) * a.ndim)
    tok = lambda n: pl.BlockSpec((tm, n), lambda i: (i, 0))
    tokt = lambda n: pl.BlockSpec((n, tm), lambda i: (0, i))
    view = lambda a: pl.BlockSpec((tm * A_WIDTH // a.shape[1], a.shape[1]), lambda i: (i, 0))
    weights = [row(ga), row(gb), w_out.astype(BF16), row(g2), wq.astype(BF16), sk]
    return pl.pallas_call(
        functools.partial(_merge_kernel, n_pat=n_pat),
        grid=(t // tm,),
        in_specs=[tok(d)] + [view(a) for a in atts] + [tok(A_WIDTH)] + [full(a) for a in weights],
        out_specs=[tok(d), tokt(d), tokt(nq)],
        out_shape=[
            jax.ShapeDtypeStruct((t, d), F32),
            jax.ShapeDtypeStruct((d, t), BF16),
            jax.ShapeDtypeStruct((nq, t), F32),
        ],
        scratch_shapes=[pltpu.VMEM((A_WIDTH // LANES, tm, LANES), F32)] * (2 * n_pat),
        compiler_params=_params("parallel"),
        name="merge_prompt" if n_pat else "merge_sample",
    )(x, *atts, gm, *weights)


def _merge_exchange(n):
    pairs, t = [], n.bit_length() - 1
    p = 1 << (t - 1)
    while p:
        q, r, d = 1 << (t - 1), 0, p
        while d:
            pairs += [(i, i + d) for i in range(n - d) if (i & p) == r]
            d, q, r = q - p, q // 2, p
        p //= 2
    return pairs


_SORT_TOPK = _merge_exchange(PEER_TOPK)


def _exchange(x, i, j):
    x[i], x[j] = jnp.maximum(x[i], x[j]), jnp.minimum(x[i], x[j])


def _sublane_all(op, x):
    for shift in (4, 2, 1):
        x = op(x, pltpu.roll(x, shift, 0))
    return x


def _sorted_top16(vals):
    x = list(vals)
    for i, j in _SORT_TOPK:
        _exchange(x, i, j)
    for shift in (4, 2, 1):
        y = [pltpu.roll(v, shift, 0) for v in x]
        x = [jnp.maximum(x[r], y[PEER_TOPK - 1 - r]) for r in range(PEER_TOPK)]
        stride = PEER_TOPK // 2
        while stride:
            for i in range(PEER_TOPK):
                if not i & stride:
                    _exchange(x, i, i + stride)
            stride //= 2
    return x


_CAND_ROWS = ((0, 0, 8), (0, 8, 8), (1, 0, 8), (2, 0, 5), (3, 0, 4), (4, 0, 3), (5, 0, 2), (6, 0, 2), (7, 0, 2))


def _select_fast(a_vals, b_vals):
    k = PEER_TOPK
    one = lambda cond: jnp.where(cond, 1.0, 0.0)
    ta, tb = _sorted_top16(a_vals), _sorted_top16(b_vals)
    sub = lax.broadcasted_iota(jnp.int32, ta[0].shape, 0)
    spread = lambda t, off: functools.reduce(lambda acc, s: jnp.where(sub == s, t[off + s], acc), range(1, 8), t[off])
    tb_lo, tb_hi, ta_hi = spread(tb, 0), spread(tb, 8), spread(ta, 8)
    cands = []
    for k1, k2_0, n_k2 in _CAND_ROWS:
        c = ta[k1] + (tb_hi if k2_0 else tb_lo)
        cands.append(c if n_k2 == 8 else jnp.where(sub < n_k2, c, NEG_INF))
    cands.append(ta_hi + tb[0])
    rem = list(cands)
    for _ in range(k):
        m = _sublane_all(jnp.maximum, functools.reduce(jnp.maximum, rem))
        rem = [jnp.where(c == m, NEG_INF, c) for c in rem]
    chosen = [one(r != c) for r, c in zip(rem, cands)]
    n_chosen = _sublane_all(jnp.add, functools.reduce(jnp.add, chosen))
    top = ta[0] + tb[0]
    z = _sublane_all(jnp.add, functools.reduce(jnp.add, [ch * jnp.exp(c - top) for ch, c in zip(chosen, cands)]))
    n_k1 = [_sublane_all(jnp.add, chosen[0] + chosen[1])] + [_sublane_all(jnp.add, chosen[i]) for i in range(2, 9)]
    n_k1 += [_sublane_all(jnp.add, jnp.where(sub == s, chosen[9], 0.0)) for s in range(8)]
    inv_z = 1.0 / z
    n_at, ea, r2, eb = [], [], [], []
    cnt_a = cnt_b = None
    for av, bv in zip(a_vals, b_vals):
        n = jnp.zeros_like(av)
        for r in range(k):
            n = jnp.where(av == ta[r], n_k1[r], n)
        n_at.append(n)
        ea.append(jnp.exp(av - ta[0]) * inv_z)
        rank = jnp.zeros_like(bv)
        for r in range(k):
            rank = jnp.where(tb[r] > bv, r + 1.0, rank)
        r2.append(rank)
        eb.append(jnp.exp(bv - tb[0]))
        ca, cb = one(av >= ta[k - 1]), one(bv >= tb[k - 1])
        cnt_a, cnt_b = (ca, cb) if cnt_a is None else (cnt_a + ca, cnt_b + cb)
    flag = one(n_chosen != k) + one(_sublane_all(jnp.add, cnt_a) != k) + one(_sublane_all(jnp.add, cnt_b) != k)
    for r in range(k - 1):
        flag = flag + one(ta[r] == ta[r + 1]) + one(tb[r] == tb[r + 1])
    return n_at, ea, r2, eb, flag


def _top16(a, iota_k):
    rem = a
    rank = jnp.full(a.shape, PEER_TOPK, jnp.int32)
    vals = []
    for r in range(PEER_TOPK):
        m = jnp.max(rem, axis=0, keepdims=True)
        first = jnp.min(jnp.where(rem == m, iota_k, PEER_NKEYS), axis=0, keepdims=True)
        sel = iota_k == first
        rank = jnp.where(sel, r, rank)
        rem = jnp.where(sel, NEG_INF, rem)
        vals.append(m)
    return jnp.concatenate(vals, axis=0), rank


def _select_exact(a, b):
    iota_k = lax.broadcasted_iota(jnp.int32, a.shape, 0)
    ta, r1 = _top16(a, iota_k)
    tb, r2 = _top16(b, iota_k)
    cand = jnp.concatenate([ta[k : k + 1, :] + tb for k in range(PEER_TOPK)], axis=0)
    pos = lax.broadcasted_iota(jnp.int32, cand.shape, 0)
    rem = cand
    chosen = jnp.zeros(cand.shape, F32)
    for _ in range(PEER_TOPK):
        m = jnp.max(rem, axis=0, keepdims=True)
        first = jnp.min(jnp.where(rem == m, pos, cand.shape[0]), axis=0, keepdims=True)
        sel = pos == first
        chosen = jnp.where(sel, 1.0, chosen)
        rem = jnp.where(sel, NEG_INF, rem)
    z = jnp.sum(chosen * jnp.exp(cand - cand[0:1, :]), axis=0, keepdims=True)
    n_at = jnp.zeros(a.shape, F32)
    for k in range(PEER_TOPK):
        n_k = jnp.sum(chosen[k * PEER_TOPK : (k + 1) * PEER_TOPK, :], axis=0, keepdims=True)
        n_at = jnp.where(r1 == k, n_k, n_at)
    return n_at, jnp.exp(a - ta[0:1, :]) / z, r2.astype(F32), jnp.exp(b - tb[0:1, :])


def _select_kernel(st_ref, na_ref, ea_ref, r2_ref, eb_ref):
    tk = st_ref.shape[1]
    n_tile = PEER_NKEYS // 8

    def head(h, carry):
        base = pl.multiple_of(h * (2 * PEER_NKEYS), 2 * PEER_NKEYS)
        for lt in range(tk // LANES):
            lanes = slice(lt * LANES, (lt + 1) * LANES)
            a_vals = [st_ref[pl.ds(base + 8 * v, 8), lanes] for v in range(n_tile)]
            b_vals = [st_ref[pl.ds(base + PEER_NKEYS + 8 * v, 8), lanes] for v in range(n_tile)]
            n_at, ea, r2, eb, flag = _select_fast(a_vals, b_vals)
            stack = lambda tiles: jnp.concatenate(tiles, axis=0)
            na_ref[h, :, lanes] = stack(n_at)
            ea_ref[h, :, lanes] = stack(ea)
            r2_ref[h, :, lanes] = stack(r2).astype(BF16)
            eb_ref[h, :, lanes] = stack(eb).astype(BF16)

            @pl.when(jnp.max(flag) > 0.0)
            def _():
                n_x, ea_x, r2_x, eb_x = _select_exact(st_ref[pl.ds(base, PEER_NKEYS), lanes], st_ref[pl.ds(base + PEER_NKEYS, PEER_NKEYS), lanes])
                na_ref[h, :, lanes] = n_x
                ea_ref[h, :, lanes] = ea_x
                r2_ref[h, :, lanes] = r2_x.astype(BF16)
                eb_ref[h, :, lanes] = eb_x.astype(BF16)

        return carry

    lax.fori_loop(0, PEER_HEADS, head, 0)


def _select(st):
    t = st.shape[1]
    tk = min(t, 256)
    out = pl.BlockSpec((PEER_HEADS, PEER_NKEYS, tk), lambda i: (0, 0, i))
    return pl.pallas_call(
        _select_kernel,
        grid=(t // tk,),
        in_specs=[pl.BlockSpec((2 * PEER_HEADS * PEER_NKEYS, tk), lambda i: (0, i))],
        out_specs=[out] * 4,
        out_shape=[jax.ShapeDtypeStruct((PEER_HEADS, PEER_NKEYS, t), dt) for dt in (F32, F32, BF16, BF16)],
        compiler_params=_params("parallel"),
        name="peer_select",
    )(st)


def _peer_kernel(u_ref, xt_ref, vt_ref, na_ref, ea_ref, r2_ref, eb_ref, o_ref, acc_ref, gt_ref):
    j = pl.program_id(1)

    @pl.when(j == 0)
    def _():
        acc_ref[...] = jnp.zeros_like(acc_ref)

    tm = xt_ref.shape[1]
    sub = BF16_SUBLANES
    row_tile = lambda ref, h, ii: jnp.broadcast_to(ref[h, ii : ii + 1, :], (sub, tm)).astype(BF16)[None]
    tiles = lambda a: a.reshape(PEER_NKEYS // sub, sub, tm)
    st = _dot(u_ref[...], xt_ref[...])
    for ii in range(u_ref.shape[0] // PEER_NKEYS):
        w = None
        for h in range(PEER_HEADS):
            hit = tiles(r2_ref[h]) < row_tile(na_ref, h, ii)
            t = jnp.where(hit, tiles(eb_ref[h]), jnp.zeros((), BF16)) * row_tile(ea_ref, h, ii)
            w = t if w is None else w + t
        rows = slice(ii * PEER_NKEYS, (ii + 1) * PEER_NKEYS)
        g = tiles(_gelu_tanh_packed(st[rows, :].astype(BF16))) * w
        gt_ref[rows, :] = g.reshape(PEER_NKEYS, tm)
    acc_ref[...] += _dot(vt_ref[...], gt_ref[...])

    @pl.when(j == pl.num_programs(1) - 1)
    def _():
        o_ref[...] = acc_ref[...].T


def _peer(xt, u, vt, na, ea, r2, eb):
    d, t = xt.shape
    e = u.shape[0]
    tm = min(t, 1024)
    te = 1024
    i1_per = te // PEER_NKEYS
    sel_i1 = pl.BlockSpec((PEER_HEADS, i1_per, tm), lambda i, j: (0, j, i))
    sel_i2 = pl.BlockSpec((PEER_HEADS, PEER_NKEYS, tm), lambda i, j: (0, 0, i))
    return pl.pallas_call(
        _peer_kernel,
        grid=(t // tm, e // te),
        in_specs=[
            pl.BlockSpec((te, d), lambda i, j: (j, 0)),
            pl.BlockSpec((d, tm), lambda i, j: (0, i)),
            pl.BlockSpec((d, te), lambda i, j: (0, j)),
            sel_i1,
            sel_i1,
            sel_i2,
            sel_i2,
        ],
        out_specs=pl.BlockSpec((tm, d), lambda i, j: (i, 0)),
        out_shape=jax.ShapeDtypeStruct((t, d), F32),
        scratch_shapes=[pltpu.VMEM((d, tm), F32), pltpu.VMEM((te, tm), BF16)],
        compiler_params=_params("parallel", "arbitrary"),
        name="peer_dense",
    )(u, xt, vt, na, ea, r2, eb)


def _final_kernel(h_ref, po_ref, p_ref, g3_ref, wg_ref, wp_ref, y_ref):
    h = h_ref[...] + po_ref[...]
    n3 = _rms(h, g3_ref[...])
    gate = jax.nn.sigmoid(_dot(n3.astype(BF16), wg_ref[...]))
    y_ref[...] = h + gate * _dot(p_ref[...].astype(BF16), wp_ref[...])


def _final(h, po, p, g3, wg, wp):
    t, d = h.shape
    tm = min(t, ROW_TILE)
    tok = lambda n: pl.BlockSpec((tm, n), lambda i: (i, 0))
    full = lambda a: pl.BlockSpec(a.shape, lambda i: (0,) * a.ndim)
    weights = [g3.reshape(1, -1), wg.astype(BF16), wp.astype(BF16)]
    return pl.pallas_call(
        _final_kernel,
        grid=(t // tm,),
        in_specs=[tok(d), tok(d), tok(p.shape[1])] + [full(a) for a in weights],
        out_specs=tok(d),
        out_shape=jax.ShapeDtypeStruct((t, d), F32),
        compiler_params=_params("parallel"),
        name="final",
    )(h, po, p, *weights)


def kernel(x_prompt, x_sample, cache_k, cache_v, p_prompt, p_sample, norm1_g, w_in, q_norm_g, k_norm_g, v_norm_g, spatial_w, spatial_b, out_norm_a_g, out_norm_b_g, w_out, norm2_g, peer_w_query, peer_sub_keys, peer_u, peer_v, norm3_g, ple_w_gate, ple_w_proj):
    depth = norm1_g.shape[0]
    bsz, s, d = x_prompt.shape
    db, dt, _ = x_sample.shape
    buf = cache_k.shape[2]
    assert bsz == 1 and dt == 1, "one prompt sequence and single-token decode rows"
    assert buf == MAX_WINDOW and s % (PATTERNS[-1][1] * BLK) == 0 and s >= MAX_WINDOW
    assert all(window == dil * BLK for window, dil in PATTERNS)
    assert PATTERNS[0][1] == 1
    hp, hs = x_prompt[0], x_sample[:, 0]
    outs = [[] for _ in range(5)]
    for l in range(depth):
        u_bf = peer_u[l].astype(BF16)
        vt_bf = peer_v[l].astype(BF16).T
        vg = v_norm_g[l].reshape(-1)

        def tail(h, atts, gm, p):
            h1, xt, st = _merge(h, atts, gm, out_norm_a_g[l], out_norm_b_g[l], w_out[l], norm2_g[l], peer_w_query[l], peer_sub_keys[l])
            po = _peer(xt, u_bf, vt_bf, *_select(st))
            return _final(h1, po, p, norm3_g[l], ple_w_gate[l], ple_w_proj[l])

        q, k, v, gm, *views = _proj(hp, 0, norm1_g[l], w_in[l], q_norm_g[l], k_norm_g[l], vg, spatial_w[l], spatial_b[l], False)
        per_pat = [_attn_prompt(q, k, v, 1)] + [_attn_prompt(*views[3 * i : 3 * i + 3], dil) for i, (_, dil) in enumerate(PATTERNS[1:])]
        hp = tail(hp, [o for o, _ in per_pat] + [lse for _, lse in per_pat], gm, p_prompt[l, 0])
        keep = min(MAX_WINDOW, s)
        outs[0].append(k[s - keep :].reshape(1, keep, A_HEADS, HEAD_DIM))
        outs[1].append(v[s - keep :].reshape(1, keep, A_HEADS, HEAD_DIM))

        q, k, v, gm, vn = _proj(hs, PAST_LEN, norm1_g[l], w_in[l], q_norm_g[l], k_norm_g[l], vg, spatial_w[l], spatial_b[l], True)
        att = _attn_sample(q, k, v, cache_k[l], cache_v[l])
        hs = tail(hs, [att], gm, p_sample[l, :, 0])
        for o, a in zip(outs[2:], (k, v, vn)):
            o.append(a.reshape(db, 1, A_HEADS, HEAD_DIM))
    return (hp[None], hs[:, None], *(jnp.stack(o) for o in outs))
```
